```python
import math
import jax, jax.numpy as jnp
from jax import lax
import numpy as np

D_MODEL = 1024
BATCH = 2
SEQ = 16384
DEPTH = 2

D_FF = 2816
D_MIX = D_MODEL
HEAD_DIM = 64
A_HEADS = 4
A_DIM = A_HEADS * HEAD_DIM
CHUNK = 128
B_GROUPS = 4
B_DIM = B_GROUPS * HEAD_DIM
B_CONV = 3
C_HEADS = 8
C_DIM = C_HEADS * HEAD_DIM
C_CONV = 4
RG_C = 8.0
IN_COLS = 2 * A_DIM + 3 * B_DIM + 2 * C_DIM
ALPHA = (2.0 * DEPTH) ** 0.25
BETA = (8.0 * DEPTH) ** -0.25
LN_EPS = 1e-5

kernel_name = "hybrid_sgu_shortconv_rglru_macaron_deepnorm"


def layer_norm(x, g, b):
    xf = x.astype(jnp.float32)
    mu = jnp.mean(xf, axis=-1, keepdims=True)
    xc = xf - mu
    var = jnp.mean(xc * xc, axis=-1, keepdims=True)
    y = xc * lax.rsqrt(var + LN_EPS) * g.astype(jnp.float32) + b.astype(jnp.float32)
    return y.astype(x.dtype)


def swiglu(x, w_gate, w_up, w_down):
    return (jax.nn.silu(x @ w_gate) * (x @ w_up)) @ w_down


def causal_dwconv(x, w):
    k_width = w.shape[0]
    s = x.shape[1]
    xp = jnp.pad(x, ((0, 0), (k_width - 1, 0), (0, 0)))
    y = xp[:, 0:s] * w[0]
    for k in range(1, k_width):
        y = y + xp[:, k:k + s] * w[k]
    return y


def mixer_spatial_gating(z, ln_g, ln_b, w_s, b_s):
    z = jax.nn.gelu(z)
    u, v = jnp.split(z, 2, axis=-1)
    v = layer_norm(v, ln_g, ln_b)
    bsz, s, _ = v.shape
    vc = v.reshape(bsz, s // CHUNK, CHUNK, A_HEADS, HEAD_DIM)
    mask = jnp.tril(jnp.ones((CHUNK, CHUNK), dtype=bool))
    w = jnp.where(mask[None], w_s, jnp.zeros_like(w_s))
    mixed = jnp.einsum('hts,bnshd->bnthd', w, vc) + b_s.T[None, None, :, :, None]
    return u * mixed.reshape(bsz, s, A_DIM)


def mixer_short_conv(z, conv_w):
    b_gate, c_gate, xin = jnp.split(z, 3, axis=-1)
    return b_gate * causal_dwconv(c_gate * xin, conv_w)


def _lru_combine(left, right):
    a1, b1 = left
    a2, b2 = right
    return a1 * a2, a2 * b1 + b2


def mixer_rglru(z, conv_w, conv_b, w_a, b_a, w_i, b_i, lam):
    gate, xr = jnp.split(z, 2, axis=-1)
    xr = causal_dwconv(xr, conv_w) + conv_b
    bsz, s, _ = xr.shape
    xh = xr.reshape(bsz, s, C_HEADS, HEAD_DIM)
    r = jax.nn.sigmoid(jnp.einsum('bshd,hde->bshe', xh, w_a) + b_a).reshape(bsz, s, C_DIM)
    i = jax.nn.sigmoid(jnp.einsum('bshd,hde->bshe', xh, w_i) + b_i).reshape(bsz, s, C_DIM)
    log_a = -RG_C * r.astype(jnp.float32) * jax.nn.softplus(-lam.astype(jnp.float32))
    a = jnp.exp(log_a)
    mult = jnp.sqrt(jnp.maximum(1.0 - jnp.exp(2.0 * log_a), 0.0))
    bx = mult * (i * xr).astype(jnp.float32)
    _, h = lax.associative_scan(_lru_combine, (a, bx), axis=1)
    return jax.nn.gelu(gate) * h.astype(gate.dtype)


def setup_inputs(seed: int = 0) -> dict:
    key = jax.random.key(seed)
    ks = jax.random.split(key, 24)

    def nrm(k, shape, fan_in, scale=1.0):
        return jax.random.normal(k, shape, jnp.float32) * (scale * fan_in ** -0.5)

    x = jax.random.normal(ks[0], (BATCH, SEQ, D_MODEL), jnp.float32)
    ln_g = 1.0 + 0.02 * jax.random.normal(ks[1], (DEPTH, 3, D_MODEL), jnp.float32)
    ln_b = 0.02 * jax.random.normal(ks[2], (DEPTH, 3, D_MODEL), jnp.float32)
    ffn_w_gate = nrm(ks[3], (DEPTH, 2, D_MODEL, D_FF), D_MODEL)
    ffn_w_up = nrm(ks[4], (DEPTH, 2, D_MODEL, D_FF), D_MODEL)
    ffn_w_down = nrm(ks[5], (DEPTH, 2, D_FF, D_MODEL), D_FF, BETA)
    w_in = nrm(ks[6], (DEPTH, D_MODEL, IN_COLS), D_MODEL)
    sgu_ln_g = 1.0 + 0.02 * jax.random.normal(ks[7], (DEPTH, A_DIM), jnp.float32)
    sgu_ln_b = 0.02 * jax.random.normal(ks[8], (DEPTH, A_DIM), jnp.float32)
    sgu_w = nrm(ks[9], (DEPTH, A_HEADS, CHUNK, CHUNK), CHUNK)
    sgu_b = 1.0 + 0.02 * jax.random.normal(ks[10], (DEPTH, A_HEADS, CHUNK), jnp.float32)
    sconv_w = nrm(ks[11], (DEPTH, B_CONV, B_DIM), B_CONV)
    rg_conv_w = nrm(ks[12], (DEPTH, C_CONV, C_DIM), C_CONV)
    rg_conv_b = 0.02 * jax.random.normal(ks[13], (DEPTH, C_DIM), jnp.float32)
    rg_w_a = nrm(ks[14], (DEPTH, C_HEADS, HEAD_DIM, HEAD_DIM), HEAD_DIM)
    rg_b_a = 0.02 * jax.random.normal(ks[15], (DEPTH, C_HEADS, HEAD_DIM), jnp.float32)
    rg_w_i = nrm(ks[16], (DEPTH, C_HEADS, HEAD_DIM, HEAD_DIM), HEAD_DIM)
    rg_b_i = 0.02 * jax.random.normal(ks[17], (DEPTH, C_HEADS, HEAD_DIM), jnp.float32)
    u = jax.random.uniform(ks[18], (DEPTH, C_DIM), jnp.float32, minval=0.9, maxval=0.999)
    a0 = u ** (1.0 / RG_C)
    rg_lambda = jnp.log(a0) - jnp.log1p(-a0)
    w_out = nrm(ks[19], (DEPTH, D_MIX, D_MODEL), D_MIX, BETA)
    return {"x": x, "ln_g": ln_g, "ln_b": ln_b, "ffn_w_gate": ffn_w_gate, "ffn_w_up": ffn_w_up,
            "ffn_w_down": ffn_w_down, "w_in": w_in, "sgu_ln_g": sgu_ln_g, "sgu_ln_b": sgu_ln_b,
            "sgu_w": sgu_w, "sgu_b": sgu_b, "sconv_w": sconv_w, "rg_conv_w": rg_conv_w,
            "rg_conv_b": rg_conv_b, "rg_w_a": rg_w_a, "rg_b_a": rg_b_a, "rg_w_i": rg_w_i,
            "rg_b_i": rg_b_i, "rg_lambda": rg_lambda, "w_out": w_out}


def reference(x, ln_g, ln_b, ffn_w_gate, ffn_w_up, ffn_w_down, w_in, sgu_ln_g, sgu_ln_b,
              sgu_w, sgu_b, sconv_w, rg_conv_w, rg_conv_b, rg_w_a, rg_b_a, rg_w_i, rg_b_i,
              rg_lambda, w_out):
    split_a = 2 * A_DIM
    split_b = split_a + 3 * B_DIM
    for l in range(DEPTH):
        f = swiglu(x, ffn_w_gate[l, 0], ffn_w_up[l, 0], ffn_w_down[l, 0])
        x = layer_norm(ALPHA * x + 0.5 * f, ln_g[l, 0], ln_b[l, 0])
        z = x @ w_in[l]
        z_a = z[..., :split_a]
        z_b = z[..., split_a:split_b]
        z_c = z[..., split_b:]
        y_a = mixer_spatial_gating(z_a, sgu_ln_g[l], sgu_ln_b[l], sgu_w[l], sgu_b[l])
        y_b = mixer_short_conv(z_b, sconv_w[l])
        y_c = mixer_rglru(z_c, rg_conv_w[l], rg_conv_b[l], rg_w_a[l], rg_b_a[l],
                          rg_w_i[l], rg_b_i[l], rg_lambda[l])
        y = jnp.concatenate([y_a, y_b, y_c], axis=-1) @ w_out[l]
        x = layer_norm(ALPHA * x + y, ln_g[l, 1], ln_b[l, 1])
        f = swiglu(x, ffn_w_gate[l, 1], ffn_w_up[l, 1], ffn_w_down[l, 1])
        x = layer_norm(ALPHA * x + 0.5 * f, ln_g[l, 2], ln_b[l, 2])
    return x
```

```python
import functools

import jax
import jax.numpy as jnp
from jax import lax
from jax.experimental import pallas as pl
from jax.experimental.pallas import tpu as pltpu

D_MODEL = 1024
D_FF = 2816
HEAD_DIM = 64
A_HEADS = 4
A_DIM = A_HEADS * HEAD_DIM
CHUNK = 128
B_DIM = 256
B_CONV = 3
C_HEADS = 8
C_DIM = C_HEADS * HEAD_DIM
C_CONV = 4
RG_C = 8.0
IN_COLS = 2 * A_DIM + 3 * B_DIM + 2 * C_DIM
SPLIT_A = 2 * A_DIM
SPLIT_B = SPLIT_A + 3 * B_DIM
LN_EPS = 1e-5

SUBLANES = 8
FFN_TILE = 512
FFN_COLS = 256
MIX_TILE = 512
VMEM_LIMIT = 56 * 1024 * 1024

F32 = jnp.float32
BF16 = jnp.bfloat16


def _layer_norm(y, g, b):
    mu = jnp.mean(y, axis=-1, keepdims=True)
    yc = y - mu
    var = jnp.mean(yc * yc, axis=-1, keepdims=True)
    return yc * lax.rsqrt(var + LN_EPS) * g + b


def _resident(shape, index_map):
    return pl.BlockSpec(shape, index_map, pipeline_mode=pl.Buffered(1))


def _ffn_kernel(alpha, x_ref, wg_ref, wu_ref, wd_ref, g_ref, b_ref, o_ref, h_ref):
    x = x_ref[...]
    xb = x.astype(BF16)
    for c in range(0, D_FF, FFN_COLS):
        hg = jnp.dot(xb, wg_ref[:, c:c + FFN_COLS], preferred_element_type=F32)
        hu = jnp.dot(xb, wu_ref[:, c:c + FFN_COLS], preferred_element_type=F32)
        h_ref[:, c:c + FFN_COLS] = (hg * jax.nn.sigmoid(hg) * hu).astype(BF16)
    f = jnp.dot(h_ref[...], wd_ref[...], preferred_element_type=F32)
    o_ref[...] = _layer_norm(alpha * x + 0.5 * f, g_ref[...], b_ref[...])


def _ffn(x, wg, wu, wd, ln_g, ln_b, layer, which, ln_idx, alpha):
    n = x.shape[0]
    assert n % FFN_TILE == 0
    depth3 = ln_g.shape[0]
    del depth3
    row = lambda i: (i, 0)
    wsel = lambda i: (layer, which, 0, 0)
    lsel = lambda i: (ln_idx, 0, 0)
    return pl.pallas_call(
        functools.partial(_ffn_kernel, alpha),
        grid=(n // FFN_TILE,),
        in_specs=[
            pl.BlockSpec((FFN_TILE, D_MODEL), row),
            _resident((None, None, D_MODEL, D_FF), wsel),
            _resident((None, None, D_MODEL, D_FF), wsel),
            _resident((None, None, D_FF, D_MODEL), wsel),
            _resident((None, 1, D_MODEL), lsel),
            _resident((None, 1, D_MODEL), lsel),
        ],
        out_specs=pl.BlockSpec((FFN_TILE, D_MODEL), row),
        out_shape=jax.ShapeDtypeStruct((n, D_MODEL), F32),
        scratch_shapes=[pltpu.VMEM((FFN_TILE, D_FF), BF16)],
        compiler_params=pltpu.CompilerParams(
            dimension_semantics=("arbitrary",), vmem_limit_bytes=VMEM_LIMIT),
        name="ffn",
    )(x, wg, wu, wd, ln_g, ln_b)


def _mixer_kernel(alpha, tiles_per_seq,
                  x_ref, win_ref, wout_ref, g_ref, b_ref, sg_ref, sb_ref, sw_ref, sbias_ref,
                  scw_ref, rcw_ref, rcb_ref, wgate_ref, bgate_ref, lam_ref,
                  o_ref,
                  wst_ref, cx_ref, xr_ref, a_ref, h_ref, gg_ref, hc_ref, y_ref):
    tm = x_ref.shape[0]
    i = pl.program_id(0)

    @pl.when(i == 0)
    def _():
        t = lax.broadcasted_iota(jnp.int32, (A_HEADS * CHUNK, CHUNK), 0) % CHUNK
        s = lax.broadcasted_iota(jnp.int32, (A_HEADS * CHUNK, CHUNK), 1)
        wst_ref[...] = jnp.where(s <= t, sw_ref[...], 0.0).astype(BF16)

    @pl.when(i % tiles_per_seq == 0)
    def _():
        cx_ref[0:SUBLANES, :] = jnp.zeros((SUBLANES, B_DIM), F32)
        xr_ref[0:SUBLANES, :] = jnp.zeros((SUBLANES, C_DIM), F32)
        hc_ref[...] = jnp.zeros_like(hc_ref)

    x = x_ref[...]
    xb = x.astype(BF16)

    za = jax.nn.gelu(jnp.dot(xb, win_ref[:, 0:SPLIT_A], preferred_element_type=F32))
    u = za[:, :A_DIM]
    vb = _layer_norm(za[:, A_DIM:], sg_ref[...], sb_ref[...]).astype(BF16)
    lane = lax.broadcasted_iota(jnp.int32, (CHUNK, A_DIM), 1)
    for c in range(0, tm, CHUNK):
        m = jnp.dot(wst_ref[...], vb[c:c + CHUNK, :], preferred_element_type=F32)
        mixed = m[0:CHUNK]
        for h in range(1, A_HEADS):
            mixed = jnp.where(lane >= h * HEAD_DIM, m[h * CHUNK:(h + 1) * CHUNK], mixed)
        mixed = mixed + sbias_ref[...]
        y_ref[c:c + CHUNK, 0:A_DIM] = (u[c:c + CHUNK] * mixed).astype(BF16)

    zb = jnp.dot(xb, win_ref[:, SPLIT_A:SPLIT_B], preferred_element_type=F32)
    cx_ref[SUBLANES:SUBLANES + tm, :] = zb[:, B_DIM:2 * B_DIM] * zb[:, 2 * B_DIM:]
    scw = scw_ref[...]
    conv = scw[B_CONV - 1:B_CONV] * cx_ref[SUBLANES:SUBLANES + tm, :]
    for k in range(B_CONV - 1):
        off = SUBLANES - (B_CONV - 1) + k
        conv = conv + scw[k:k + 1] * cx_ref[off:off + tm, :]
    y_ref[:, A_DIM:A_DIM + B_DIM] = (zb[:, :B_DIM] * conv).astype(BF16)
    cx_ref[0:SUBLANES, :] = cx_ref[tm:tm + SUBLANES, :]

    zc = jnp.dot(xb, win_ref[:, SPLIT_B:], preferred_element_type=F32)
    gg_ref[...] = jax.nn.gelu(zc[:, :C_DIM])
    xr_ref[SUBLANES:SUBLANES + tm, :] = zc[:, C_DIM:]
    rcw = rcw_ref[...]
    xc = rcw[C_CONV - 1:C_CONV] * xr_ref[SUBLANES:SUBLANES + tm, :] + rcb_ref[...]
    for k in range(C_CONV - 1):
        off = SUBLANES - (C_CONV - 1) + k
        xc = xc + rcw[k:k + 1] * xr_ref[off:off + tm, :]
    xr_ref[0:SUBLANES, :] = xr_ref[tm:tm + SUBLANES, :]
    xcb = xc.astype(BF16)
    lam = lam_ref[...]
    decay = -RG_C * jax.nn.softplus(-lam)
    half = C_DIM // 2
    for hf in range(2):
        cols = slice(hf * half, (hf + 1) * half)
        gz = jnp.dot(xcb[:, cols], wgate_ref[hf], preferred_element_type=F32) + bgate_ref[hf]
        r = jax.nn.sigmoid(gz[:, :half])
        ig = jax.nn.sigmoid(gz[:, half:])
        a = jnp.exp(r * decay[:, cols])
        mult = jnp.sqrt(jnp.maximum(1.0 - a * a, 0.0))
        a_ref[:, cols] = a
        h_ref[:, cols] = mult * (ig * xc[:, cols])

    rowid = lax.broadcasted_iota(jnp.int32, (SUBLANES, C_DIM), 0)

    def scan_group(g, h_prev):
        r0 = pl.multiple_of(g * SUBLANES, SUBLANES)
        a = a_ref[pl.ds(r0, SUBLANES), :]
        b = h_ref[pl.ds(r0, SUBLANES), :]
        for d in (1, 2, 4):
            keep = rowid >= d
            a_up = jnp.where(keep, pltpu.roll(a, d, axis=0), 1.0)
            b_up = jnp.where(keep, pltpu.roll(b, d, axis=0), 0.0)
            b = a * b_up + b
            a = a * a_up
        h = a * h_prev + b
        h_ref[pl.ds(r0, SUBLANES), :] = h
        return h[SUBLANES - 1:SUBLANES, :]

    hc_ref[...] = lax.fori_loop(0, tm // SUBLANES, scan_group, hc_ref[...], unroll=8)
    y_ref[:, A_DIM + B_DIM:] = (gg_ref[...] * h_ref[...]).astype(BF16)

    y = jnp.dot(y_ref[...], wout_ref[...], preferred_element_type=F32)
    o_ref[...] = _layer_norm(alpha * x + y, g_ref[...], b_ref[...])


def _mixer(x, seq_len, layer, ln_idx, alpha, w_in, w_out, ln_g, ln_b, sgu_ln_g, sgu_ln_b, sgu_w,
           sgu_bias, sconv_w, rg_conv_w, rg_conv_b, w_gate, b_gate, rg_lambda):
    n = x.shape[0]
    tm = MIX_TILE
    assert seq_len % tm == 0 and tm % CHUNK == 0
    row = lambda i: (i, 0)
    l2 = lambda i: (layer, 0, 0)
    l3 = lambda i: (layer, 0, 0, 0)
    lsel = lambda i: (ln_idx, 0, 0)
    half = C_DIM // 2
    return pl.pallas_call(
        functools.partial(_mixer_kernel, alpha, seq_len // tm),
        grid=(n // tm,),
        in_specs=[
            pl.BlockSpec((tm, D_MODEL), row),
            _resident((None, D_MODEL, IN_COLS), l2),
            _resident((None, D_MODEL, D_MODEL), l2),
            _resident((None, 1, D_MODEL), lsel),
            _resident((None, 1, D_MODEL), lsel),
            _resident((None, 1, A_DIM), l2),
            _resident((None, 1, A_DIM), l2),
            _resident((None, A_HEADS * CHUNK, CHUNK), l2),
            _resident((None, CHUNK, A_DIM), l2),
            _resident((None, B_CONV, B_DIM), l2),
            _resident((None, C_CONV, C_DIM), l2),
            _resident((None, 1, C_DIM), l2),
            _resident((None, 2, half, 2 * half), l3),
            _resident((None, 2, 1, 2 * half), l3),
            _resident((None, 1, C_DIM), l2),
        ],
        out_specs=pl.BlockSpec((tm, D_MODEL), row),
        out_shape=jax.ShapeDtypeStruct((n, D_MODEL), F32),
        scratch_shapes=[
            pltpu.VMEM((A_HEADS * CHUNK, CHUNK), BF16),
            pltpu.VMEM((tm + SUBLANES, B_DIM), F32),
            pltpu.VMEM((tm + SUBLANES, C_DIM), F32),
            pltpu.VMEM((tm, C_DIM), F32),
            pltpu.VMEM((tm, C_DIM), F32),
            pltpu.VMEM((tm, C_DIM), F32),
            pltpu.VMEM((1, C_DIM), F32),
            pltpu.VMEM((tm, D_MODEL), BF16),
        ],
        compiler_params=pltpu.CompilerParams(
            dimension_semantics=("arbitrary",), vmem_limit_bytes=VMEM_LIMIT),
        name="mixer",
    )(x, w_in, w_out, ln_g, ln_b, sgu_ln_g, sgu_ln_b, sgu_w, sgu_bias, sconv_w, rg_conv_w,
      rg_conv_b, w_gate, b_gate, rg_lambda)


def _block_diag_gates(w_a, w_i, b_a, b_i):
    depth = w_a.shape[0]
    half_heads = C_HEADS // 2
    eye = jnp.eye(half_heads, dtype=w_a.dtype)

    def bd(w):
        w = w.reshape(depth, 2, half_heads, HEAD_DIM, HEAD_DIM)
        full = jnp.einsum("lghde,hk->lghdke", w, eye)
        return full.reshape(depth, 2, half_heads * HEAD_DIM, half_heads * HEAD_DIM)

    w = jnp.concatenate([bd(w_a), bd(w_i)], axis=-1)
    half = C_DIM // 2
    b = jnp.concatenate([b_a.reshape(depth, 2, 1, half), b_i.reshape(depth, 2, 1, half)], axis=-1)
    return w, b


def kernel(x, ln_g, ln_b, ffn_w_gate, ffn_w_up, ffn_w_down, w_in, sgu_ln_g, sgu_ln_b, sgu_w, sgu_b,
           sconv_w, rg_conv_w, rg_conv_b, rg_w_a, rg_b_a, rg_w_i, rg_b_i, rg_lambda, w_out):
    batch, seq_len, _ = x.shape
    depth = w_in.shape[0]
    alpha = (2.0 * depth) ** 0.25

    wg, wu, wd = (w.astype(BF16) for w in (ffn_w_gate, ffn_w_up, ffn_w_down))
    w_in_b, w_out_b = w_in.astype(BF16), w_out.astype(BF16)
    ln_g3 = ln_g.reshape(depth * 3, 1, D_MODEL)
    ln_b3 = ln_b.reshape(depth * 3, 1, D_MODEL)
    sgu_w2 = sgu_w.reshape(depth, A_HEADS * CHUNK, CHUNK)
    sgu_bias = jnp.repeat(jnp.swapaxes(sgu_b, 1, 2), HEAD_DIM, axis=-1)
    w_gate, b_gate = _block_diag_gates(rg_w_a, rg_w_i, rg_b_a, rg_b_i)
    w_gate = w_gate.astype(BF16)
    row3 = lambda p: p.reshape(depth, 1, p.shape[-1])

    h = x.reshape(batch * seq_len, D_MODEL)
    for l in range(depth):
        h = _ffn(h, wg, wu, wd, ln_g3, ln_b3, l, 0, 3 * l, alpha)
        h = _mixer(h, seq_len, l, 3 * l + 1, alpha, w_in_b, w_out_b, ln_g3, ln_b3,
                   row3(sgu_ln_g), row3(sgu_ln_b), sgu_w2, sgu_bias, sconv_w, rg_conv_w,
                   row3(rg_conv_b), w_gate, b_gate, row3(rg_lambda))
        h = _ffn(h, wg, wu, wd, ln_g3, ln_b3, l, 1, 3 * l + 2, alpha)
    return h.reshape(batch, seq_len, D_MODEL)
```

```python
import functools
import math

import jax
import jax.numpy as jnp
from jax import lax
from jax.experimental import pallas as pl
from jax.experimental.pallas import tpu as pltpu

D_MODEL = 1024
D_FF = 2816
HEAD_DIM = 64
A_HEADS = 4
A_DIM = A_HEADS * HEAD_DIM
CHUNK = 128
B_DIM = 256
B_CONV = 3
C_HEADS = 8
C_DIM = C_HEADS * HEAD_DIM
C_CONV = 4
RG_C = 8.0
IN_COLS = 2 * A_DIM + 3 * B_DIM + 2 * C_DIM
SPLIT_A = 2 * A_DIM
SPLIT_B = SPLIT_A + 3 * B_DIM
LN_EPS = 1e-5

SUBLANES = 8
LANES = 128
FFN_TILE = 512
FFN_COLS = 256
MIX_TILE = 512
MIX_SEG = MIX_TILE // SUBLANES
MIX_PITCH = MIX_SEG + 4
VMEM_LIMIT = 56 * 1024 * 1024

F32 = jnp.float32
BF16 = jnp.bfloat16

_GELU_C0 = -2.0 * math.sqrt(2.0 / math.pi) * math.log2(math.e)
_GELU_C1 = _GELU_C0 * 0.044715


def _layer_norm(y, g, b):
    mu = jnp.mean(y, axis=-1, keepdims=True)
    yc = y - mu
    var = jnp.mean(yc * yc, axis=-1, keepdims=True)
    return yc * lax.rsqrt(var + LN_EPS) * g + b


def _gelu(x):
    return x / (1.0 + jnp.exp2(x * (_GELU_C0 + _GELU_C1 * (x * x))))


def _resident(shape, index_map):
    return pl.BlockSpec(shape, index_map, pipeline_mode=pl.Buffered(1))


def _ffn_kernel(alpha, x_ref, wg_ref, wu_ref, wd_ref, g_ref, b_ref, o_ref, h_ref):
    x = x_ref[...]
    xb = x.astype(BF16)
    for c in range(0, D_FF, FFN_COLS):
        hg = jnp.dot(xb, wg_ref[:, c:c + FFN_COLS], preferred_element_type=F32)
        hu = jnp.dot(xb, wu_ref[:, c:c + FFN_COLS], preferred_element_type=F32)
        h_ref[:, c:c + FFN_COLS] = (hg * jax.nn.sigmoid(hg) * hu).astype(BF16)
    f = jnp.dot(h_ref[...], wd_ref[...], preferred_element_type=F32)
    o_ref[...] = _layer_norm(alpha * x + 0.5 * f, g_ref[...], b_ref[...])


def _ffn(x, wg, wu, wd, ln_g, ln_b, layer, which, ln_idx, alpha):
    n = x.shape[0]
    assert n % FFN_TILE == 0
    depth3 = ln_g.shape[0]
    del depth3
    row = lambda i: (i, 0)
    wsel = lambda i: (layer, which, 0, 0)
    lsel = lambda i: (ln_idx, 0, 0)
    return pl.pallas_call(
        functools.partial(_ffn_kernel, alpha),
        grid=(n // FFN_TILE,),
        in_specs=[
            pl.BlockSpec((FFN_TILE, D_MODEL), row),
            _resident((None, None, D_MODEL, D_FF), wsel),
            _resident((None, None, D_MODEL, D_FF), wsel),
            _resident((None, None, D_FF, D_MODEL), wsel),
            _resident((None, 1, D_MODEL), lsel),
            _resident((None, 1, D_MODEL), lsel),
        ],
        out_specs=pl.BlockSpec((FFN_TILE, D_MODEL), row),
        out_shape=jax.ShapeDtypeStruct((n, D_MODEL), F32),
        scratch_shapes=[pltpu.VMEM((FFN_TILE, D_FF), BF16)],
        compiler_params=pltpu.CompilerParams(
            dimension_semantics=("arbitrary",), vmem_limit_bytes=VMEM_LIMIT),
        name="ffn",
    )(x, wg, wu, wd, ln_g, ln_b)


def _store_segments(ref, val, col0):
    for q in range(val.shape[1] // LANES):
        for s in range(SUBLANES):
            ref[col0 // LANES + q, s * MIX_PITCH:s * MIX_PITCH + MIX_SEG, :] = (
                val[s * MIX_SEG:(s + 1) * MIX_SEG, q * LANES:(q + 1) * LANES])


def _rglru_scan(a_ref, bx_ref, gg_ref, yc_ref, carry_ref):
    rowid = lax.broadcasted_iota(jnp.int32, (SUBLANES, LANES), 0)

    def rows(ref, q, j):
        return ref.at[q][pl.ds(j, SUBLANES, stride=MIX_PITCH), :]

    for q in range(C_DIM // LANES):
        lanes = slice(q * LANES, (q + 1) * LANES)
        p = rows(a_ref, q, 0)
        h = rows(bx_ref, q, 0)
        for j in range(1, MIX_SEG):
            a = rows(a_ref, q, j)
            h = a * h + rows(bx_ref, q, j)
            p = a * p
        for d in (1, 2, 4):
            keep = rowid >= d
            p_up = jnp.where(keep, pltpu.roll(p, d, axis=0), 1.0)
            h_up = jnp.where(keep, pltpu.roll(h, d, axis=0), 0.0)
            h = p * h_up + h
            p = p * p_up
        prev = carry_ref[:, lanes]
        end = p * prev + h
        carry_ref[:, lanes] = jnp.broadcast_to(end[SUBLANES - 1:SUBLANES, :], (SUBLANES, LANES))
        h = jnp.where(rowid >= 1, pltpu.roll(end, 1, axis=0), prev)
        for j in range(MIX_SEG):
            h = rows(a_ref, q, j) * h + rows(bx_ref, q, j)
            yc_ref.at[q][pl.ds(j, SUBLANES, stride=MIX_PITCH), :] = h * rows(gg_ref, q, j)


def _mixer_kernel(alpha, tiles_per_seq,
                  x_ref, win_ref, wout_ref, g_ref, b_ref, sg_ref, sb_ref, sw_ref, sbias_ref,
                  scw_ref, rcw_ref, rcb_ref, wgate_ref, bgate_ref, lam_ref,
                  o_ref,
                  wst_ref, cx_ref, xr_ref, a_ref, bx_ref, gg_ref, yc_ref, hc_ref, y_ref):
    tm = x_ref.shape[0]
    i = pl.program_id(0)

    @pl.when(i == 0)
    def _():
        t = lax.broadcasted_iota(jnp.int32, (A_HEADS * CHUNK, CHUNK), 0) % CHUNK
        s = lax.broadcasted_iota(jnp.int32, (A_HEADS * CHUNK, CHUNK), 1)
        wst_ref[...] = jnp.where(s <= t, sw_ref[...], 0.0).astype(BF16)

    @pl.when(i % tiles_per_seq == 0)
    def _():
        cx_ref[0:SUBLANES, :] = jnp.zeros((SUBLANES, B_DIM), F32)
        xr_ref[0:SUBLANES, :] = jnp.zeros((SUBLANES, C_DIM), F32)
        hc_ref[...] = jnp.zeros_like(hc_ref)

    x = x_ref[...]
    xb = x.astype(BF16)

    za = _gelu(jnp.dot(xb, win_ref[:, 0:SPLIT_A], preferred_element_type=F32))
    u = za[:, :A_DIM]
    vb = _layer_norm(za[:, A_DIM:], sg_ref[...], sb_ref[...]).astype(BF16)
    lane = lax.broadcasted_iota(jnp.int32, (CHUNK, A_DIM), 1)
    for c in range(0, tm, CHUNK):
        m = jnp.dot(wst_ref[...], vb[c:c + CHUNK, :], preferred_element_type=F32)
        mixed = m[0:CHUNK]
        for h in range(1, A_HEADS):
            mixed = jnp.where(lane >= h * HEAD_DIM, m[h * CHUNK:(h + 1) * CHUNK], mixed)
        mixed = mixed + sbias_ref[...]
        y_ref[c:c + CHUNK, 0:A_DIM] = (u[c:c + CHUNK] * mixed).astype(BF16)

    zb = jnp.dot(xb, win_ref[:, SPLIT_A:SPLIT_B], preferred_element_type=F32)
    cx_ref[SUBLANES:SUBLANES + tm, :] = zb[:, B_DIM:2 * B_DIM] * zb[:, 2 * B_DIM:]
    scw = scw_ref[...]
    conv = scw[B_CONV - 1:B_CONV] * cx_ref[SUBLANES:SUBLANES + tm, :]
    for k in range(B_CONV - 1):
        off = SUBLANES - (B_CONV - 1) + k
        conv = conv + scw[k:k + 1] * cx_ref[off:off + tm, :]
    y_ref[:, A_DIM:A_DIM + B_DIM] = (zb[:, :B_DIM] * conv).astype(BF16)
    cx_ref[0:SUBLANES, :] = cx_ref[tm:tm + SUBLANES, :]

    zc = jnp.dot(xb, win_ref[:, SPLIT_B:], preferred_element_type=F32)
    _store_segments(gg_ref, _gelu(zc[:, :C_DIM]), 0)
    xr_ref[SUBLANES:SUBLANES + tm, :] = zc[:, C_DIM:]
    rcw = rcw_ref[...]
    xc = rcw[C_CONV - 1:C_CONV] * xr_ref[SUBLANES:SUBLANES + tm, :] + rcb_ref[...]
    for k in range(C_CONV - 1):
        off = SUBLANES - (C_CONV - 1) + k
        xc = xc + rcw[k:k + 1] * xr_ref[off:off + tm, :]
    xr_ref[0:SUBLANES, :] = xr_ref[tm:tm + SUBLANES, :]
    xcb = xc.astype(BF16)
    lam = lam_ref[...]
    decay = -RG_C * jax.nn.softplus(-lam)
    half = C_DIM // 2
    for hf in range(2):
        cols = slice(hf * half, (hf + 1) * half)
        gz = jnp.dot(xcb[:, cols], wgate_ref[hf], preferred_element_type=F32) + bgate_ref[hf]
        r = jax.nn.sigmoid(gz[:, :half])
        ig = jax.nn.sigmoid(gz[:, half:])
        a = jnp.exp(r * decay[:, cols])
        mult = jnp.sqrt(jnp.maximum(1.0 - a * a, 0.0))
        _store_segments(a_ref, a, hf * half)
        _store_segments(bx_ref, mult * (ig * xc[:, cols]), hf * half)

    _rglru_scan(a_ref, bx_ref, gg_ref, yc_ref, hc_ref)
    for q in range(C_DIM // LANES):
        for s in range(SUBLANES):
            c0 = A_DIM + B_DIM + q * LANES
            y_ref[s * MIX_SEG:(s + 1) * MIX_SEG, c0:c0 + LANES] = (
                yc_ref[q, s * MIX_PITCH:s * MIX_PITCH + MIX_SEG, :].astype(BF16))

    y = jnp.dot(y_ref[...], wout_ref[...], preferred_element_type=F32)
    o_ref[...] = _layer_norm(alpha * x + y, g_ref[...], b_ref[...])


def _mixer(x, seq_len, layer, ln_idx, alpha, w_in, w_out, ln_g, ln_b, sgu_ln_g, sgu_ln_b, sgu_w,
           sgu_bias, sconv_w, rg_conv_w, rg_conv_b, w_gate, b_gate, rg_lambda):
    n = x.shape[0]
    tm = MIX_TILE
    assert seq_len % tm == 0 and tm % CHUNK == 0
    row = lambda i: (i, 0)
    l2 = lambda i: (layer, 0, 0)
    l3 = lambda i: (layer, 0, 0, 0)
    lsel = lambda i: (ln_idx, 0, 0)
    half = C_DIM // 2
    scan_buf = (C_DIM // LANES, SUBLANES * MIX_PITCH, LANES)
    return pl.pallas_call(
        functools.partial(_mixer_kernel, alpha, seq_len // tm),
        grid=(n // tm,),
        in_specs=[
            pl.BlockSpec((tm, D_MODEL), row),
            _resident((None, D_MODEL, IN_COLS), l2),
            _resident((None, D_MODEL, D_MODEL), l2),
            _resident((None, 1, D_MODEL), lsel),
            _resident((None, 1, D_MODEL), lsel),
            _resident((None, 1, A_DIM), l2),
            _resident((None, 1, A_DIM), l2),
            _resident((None, A_HEADS * CHUNK, CHUNK), l2),
            _resident((None, CHUNK, A_DIM), l2),
            _resident((None, B_CONV, B_DIM), l2),
            _resident((None, C_CONV, C_DIM), l2),
            _resident((None, 1, C_DIM), l2),
            _resident((None, 2, half, 2 * half), l3),
            _resident((None, 2, 1, 2 * half), l3),
            _resident((None, 1, C_DIM), l2),
        ],
        out_specs=pl.BlockSpec((tm, D_MODEL), row),
        out_shape=jax.ShapeDtypeStruct((n, D_MODEL), F32),
        scratch_shapes=[
            pltpu.VMEM((A_HEADS * CHUNK, CHUNK), BF16),
            pltpu.VMEM((tm + SUBLANES, B_DIM), F32),
            pltpu.VMEM((tm + SUBLANES, C_DIM), F32),
            pltpu.VMEM(scan_buf, F32),
            pltpu.VMEM(scan_buf, F32),
            pltpu.VMEM(scan_buf, F32),
            pltpu.VMEM(scan_buf, F32),
            pltpu.VMEM((SUBLANES, C_DIM), F32),
            pltpu.VMEM((tm, D_MODEL), BF16),
        ],
        compiler_params=pltpu.CompilerParams(
            dimension_semantics=("arbitrary",), vmem_limit_bytes=VMEM_LIMIT),
        name="mixer",
    )(x, w_in, w_out, ln_g, ln_b, sgu_ln_g, sgu_ln_b, sgu_w, sgu_bias, sconv_w, rg_conv_w,
      rg_conv_b, w_gate, b_gate, rg_lambda)


def _block_diag_gates(w_a, w_i, b_a, b_i):
    depth = w_a.shape[0]
    half_heads = C_HEADS // 2
    eye = jnp.eye(half_heads, dtype=w_a.dtype)

    def bd(w):
        w = w.reshape(depth, 2, half_heads, HEAD_DIM, HEAD_DIM)
        full = jnp.einsum("lghde,hk->lghdke", w, eye)
        return full.reshape(depth, 2, half_heads * HEAD_DIM, half_heads * HEAD_DIM)

    w = jnp.concatenate([bd(w_a), bd(w_i)], axis=-1)
    half = C_DIM // 2
    b = jnp.concatenate([b_a.reshape(depth, 2, 1, half), b_i.reshape(depth, 2, 1, half)], axis=-1)
    return w, b


def kernel(x, ln_g, ln_b, ffn_w_gate, ffn_w_up, ffn_w_down, w_in, sgu_ln_g, sgu_ln_b, sgu_w, sgu_b,
           sconv_w, rg_conv_w, rg_conv_b, rg_w_a, rg_b_a, rg_w_i, rg_b_i, rg_lambda, w_out):
    batch, seq_len, _ = x.shape
    depth = w_in.shape[0]
    alpha = (2.0 * depth) ** 0.25

    wg, wu, wd = (w.astype(BF16) for w in (ffn_w_gate, ffn_w_up, ffn_w_down))
    w_in_b, w_out_b = w_in.astype(BF16), w_out.astype(BF16)
    ln_g3 = ln_g.reshape(depth * 3, 1, D_MODEL)
    ln_b3 = ln_b.reshape(depth * 3, 1, D_MODEL)
    sgu_w2 = sgu_w.reshape(depth, A_HEADS * CHUNK, CHUNK)
    sgu_bias = jnp.repeat(jnp.swapaxes(sgu_b, 1, 2), HEAD_DIM, axis=-1)
    w_gate, b_gate = _block_diag_gates(rg_w_a, rg_w_i, rg_b_a, rg_b_i)
    w_gate = w_gate.astype(BF16)
    row3 = lambda p: p.reshape(depth, 1, p.shape[-1])

    h = x.reshape(batch * seq_len, D_MODEL)
    for l in range(depth):
        h = _ffn(h, wg, wu, wd, ln_g3, ln_b3, l, 0, 3 * l, alpha)
        h = _mixer(h, seq_len, l, 3 * l + 1, alpha, w_in_b, w_out_b, ln_g3, ln_b3,
                   row3(sgu_ln_g), row3(sgu_ln_b), sgu_w2, sgu_bias, sconv_w, rg_conv_w,
                   row3(rg_conv_b), w_gate, b_gate, row3(rg_lambda))
        h = _ffn(h, wg, wu, wd, ln_g3, ln_b3, l, 1, 3 * l + 2, alpha)
    return h.reshape(batch, seq_len, D_MODEL)
```

```python
import functools
import math

import jax
import jax.numpy as jnp
from jax import lax
from jax.experimental import pallas as pl
from jax.experimental.pallas import tpu as pltpu

D_MODEL = 1024
D_FF = 2816
HEAD_DIM = 64
A_HEADS = 4
A_DIM = A_HEADS * HEAD_DIM
CHUNK = 128
B_DIM = 256
B_CONV = 3
C_HEADS = 8
C_DIM = C_HEADS * HEAD_DIM
C_CONV = 4
RG_C = 8.0
IN_COLS = 2 * A_DIM + 3 * B_DIM + 2 * C_DIM
SPLIT_A = 2 * A_DIM
SPLIT_B = SPLIT_A + 3 * B_DIM
LN_EPS = 1e-5

SUBLANES = 8
LANES = 128
FFN_TILE = 512
FFN_SUB = 2
FFN_COLS = 256
MIX_TILE = 512
MIX_SUB = 2
MIX_SEG = MIX_TILE // SUBLANES
MIX_PITCH_IN = MIX_SEG + 4
MIX_PITCH_OUT = MIX_SEG + 8
LOG2E = math.log2(math.e)
VMEM_LIMIT = 60 * 1024 * 1024

F32 = jnp.float32
BF16 = jnp.bfloat16

_GELU_C0 = -2.0 * math.sqrt(2.0 / math.pi) * LOG2E
_GELU_C1 = _GELU_C0 * 0.044715


def _layer_norm(y, g, b):
    mu = jnp.mean(y, axis=-1, keepdims=True)
    yc = y - mu
    var = jnp.mean(yc * yc, axis=-1, keepdims=True)
    return yc * lax.rsqrt(var + LN_EPS) * g + b


def _gelu(x):
    return x / (1.0 + jnp.exp2(x * (_GELU_C0 + _GELU_C1 * (x * x))))


def _sigmoid(x):
    return 1.0 / (1.0 + jnp.exp2(x * (-LOG2E)))


def _resident(shape, index_map):
    return pl.BlockSpec(shape, index_map, pipeline_mode=pl.Buffered(1))


def _ffn_kernel(alpha, x_ref, wg_ref, wu_ref, wd_ref, g_ref, b_ref, o_ref, h_ref):
    for k in range(FFN_SUB):
        rows = slice(k * FFN_TILE, (k + 1) * FFN_TILE)
        x = x_ref[rows, :]
        xb = x.astype(BF16)
        for c in range(0, D_FF, FFN_COLS):
            hg = jnp.dot(xb, wg_ref[:, c:c + FFN_COLS], preferred_element_type=F32)
            hu = jnp.dot(xb, wu_ref[:, c:c + FFN_COLS], preferred_element_type=F32)
            h_ref[k, :, c:c + FFN_COLS] = (hg * _sigmoid(hg) * hu).astype(BF16)
        f = jnp.dot(h_ref[k], wd_ref[...], preferred_element_type=F32)
        o_ref[rows, :] = _layer_norm(alpha * x + 0.5 * f, g_ref[...], b_ref[...])


def _ffn(x, wg, wu, wd, ln_g, ln_b, layer, which, ln_idx, alpha):
    n = x.shape[0]
    step = FFN_TILE * FFN_SUB
    assert n % step == 0
    row = lambda i: (i, 0)
    wsel = lambda i: (layer, which, 0, 0)
    lsel = lambda i: (ln_idx, 0, 0)
    return pl.pallas_call(
        functools.partial(_ffn_kernel, alpha),
        grid=(n // step,),
        in_specs=[
            pl.BlockSpec((step, D_MODEL), row),
            _resident((None, None, D_MODEL, D_FF), wsel),
            _resident((None, None, D_MODEL, D_FF), wsel),
            _resident((None, None, D_FF, D_MODEL), wsel),
            _resident((None, 1, D_MODEL), lsel),
            _resident((None, 1, D_MODEL), lsel),
        ],
        out_specs=pl.BlockSpec((step, D_MODEL), row),
        out_shape=jax.ShapeDtypeStruct((n, D_MODEL), F32),
        scratch_shapes=[pltpu.VMEM((FFN_SUB, FFN_TILE, D_FF), BF16)],
        compiler_params=pltpu.CompilerParams(
            dimension_semantics=("arbitrary",), vmem_limit_bytes=VMEM_LIMIT),
        name="ffn",
    )(x, wg, wu, wd, ln_g, ln_b)


def _store_segments(ref, val):
    for q in range(val.shape[1] // LANES):
        for s in range(SUBLANES):
            ref[q, s * MIX_PITCH_IN:s * MIX_PITCH_IN + MIX_SEG, :] = (
                val[s * MIX_SEG:(s + 1) * MIX_SEG, q * LANES:(q + 1) * LANES])


def _segment_vector(ref, q, j):
    return ref.at[q][pl.ds(j, SUBLANES, stride=MIX_PITCH_IN), :]


def _segment_conv(xr_ref, hist_ref, w, bias, xc_ref):
    rowid = lax.broadcasted_iota(jnp.int32, (SUBLANES, LANES), 0)
    for q in range(C_DIM // LANES):
        lanes = slice(q * LANES, (q + 1) * LANES)
        wq = [w[k:k + 1, lanes] for k in range(C_CONV)]
        past = []
        for m in range(C_CONV - 1):
            cur = _segment_vector(xr_ref, q, MIX_SEG - (C_CONV - 1) + m)
            past.append(pltpu.roll(jnp.where(rowid == SUBLANES - 1, hist_ref[m, :, lanes], cur), 1, axis=0))
            hist_ref[m, :, lanes] = cur
        for j in range(MIX_SEG):
            x0 = _segment_vector(xr_ref, q, j)
            acc = wq[C_CONV - 1] * x0 + bias[:, lanes]
            for k in range(C_CONV - 1):
                acc = acc + wq[k] * past[k]
            xc_ref[j * SUBLANES:(j + 1) * SUBLANES, lanes] = acc
            past = past[1:] + [x0]


def _segment_scan(a_ref, bx_ref, gate_ref, yc_ref, carry_ref):
    rowid = lax.broadcasted_iota(jnp.int32, (SUBLANES, LANES), 0)

    for q in range(C_DIM // LANES):
        lanes = slice(q * LANES, (q + 1) * LANES)
        vec = lambda ref, j: ref[j * SUBLANES:(j + 1) * SUBLANES, lanes]
        p = vec(a_ref, 0)
        h = vec(bx_ref, 0)
        for j in range(1, MIX_SEG):
            a = vec(a_ref, j)
            h = a * h + vec(bx_ref, j)
            p = a * p
        for d in (1, 2, 4):
            keep = rowid >= d
            p_up = jnp.where(keep, pltpu.roll(p, d, axis=0), 1.0)
            h_up = jnp.where(keep, pltpu.roll(h, d, axis=0), 0.0)
            h = p * h_up + h
            p = p * p_up
        prev = carry_ref[:, lanes]
        end = p * prev + h
        carry_ref[:, lanes] = jnp.broadcast_to(end[SUBLANES - 1:SUBLANES, :], (SUBLANES, LANES))
        h = jnp.where(rowid >= 1, pltpu.roll(end, 1, axis=0), prev)
        for j in range(MIX_SEG):
            h = vec(a_ref, j) * h + vec(bx_ref, j)
            yc_ref.at[q][pl.ds(j, SUBLANES, stride=MIX_PITCH_OUT), :] = (
                h * _gelu(_segment_vector(gate_ref, q, j)))


class _SubTile:
    def __init__(self, alpha, x_ref, o_ref, k, p, s):
        self.alpha, self.o_ref, self.p, self.s = alpha, o_ref, p, s
        self.rows = slice(k * MIX_TILE, (k + 1) * MIX_TILE)
        self.x = x_ref[self.rows, :]
        self.xb = self.x.astype(BF16)
        for name in ("cx", "gate", "xr", "xc", "a", "bx", "yc", "y"):
            setattr(self, name, s[name].at[k])

    def proj_a(self):
        self.za = jnp.dot(self.xb, self.p["w_in"][:, 0:SPLIT_A], preferred_element_type=F32)

    def proj_b(self):
        self.zb = jnp.dot(self.xb, self.p["w_in"][:, SPLIT_A:SPLIT_B], preferred_element_type=F32)

    def proj_c(self):
        self.zc = jnp.dot(self.xb, self.p["w_in"][:, SPLIT_B:], preferred_element_type=F32)

    def sgu_pre(self):
        za = _gelu(self.za)
        self.u = za[:, :A_DIM]
        self.vb = _layer_norm(za[:, A_DIM:], self.p["sgu_g"][...], self.p["sgu_b"][...]).astype(BF16)

    def sgu_mix(self):
        lane = lax.broadcasted_iota(jnp.int32, (CHUNK, A_DIM), 1)
        for c in range(0, MIX_TILE, CHUNK):
            m = jnp.dot(self.s["wst"][...], self.vb[c:c + CHUNK, :], preferred_element_type=F32)
            mixed = m[0:CHUNK]
            for h in range(1, A_HEADS):
                mixed = jnp.where(lane >= h * HEAD_DIM, m[h * CHUNK:(h + 1) * CHUNK], mixed)
            mixed = mixed + self.p["sgu_bias"][...]
            self.y[c:c + CHUNK, 0:A_DIM] = (self.u[c:c + CHUNK] * mixed).astype(BF16)

    def short_conv(self):
        tm, zb, cx, tail = MIX_TILE, self.zb, self.cx, self.s["cx_tail"]
        cx[0:SUBLANES, :] = tail[...]
        cx[SUBLANES:SUBLANES + tm, :] = zb[:, B_DIM:2 * B_DIM] * zb[:, 2 * B_DIM:]
        w = self.p["sconv_w"][...]
        conv = w[B_CONV - 1:B_CONV] * cx[SUBLANES:SUBLANES + tm, :]
        for k in range(B_CONV - 1):
            off = SUBLANES - (B_CONV - 1) + k
            conv = conv + w[k:k + 1] * cx[off:off + tm, :]
        self.y[:, A_DIM:A_DIM + B_DIM] = (zb[:, :B_DIM] * conv).astype(BF16)
        tail[...] = cx[tm:tm + SUBLANES, :]

    def rg_conv(self):
        _store_segments(self.gate, self.zc[:, :C_DIM])
        _store_segments(self.xr, self.zc[:, C_DIM:])
        _segment_conv(self.xr, self.s["xr_hist"], self.p["rg_conv_w"][...], self.p["rg_conv_b"][...],
                      self.xc)

    def rg_gate_proj(self):
        half = C_DIM // 2
        xcb = self.xc[...].astype(BF16)
        self.gz = [jnp.dot(xcb[:, hf * half:(hf + 1) * half], self.p["w_gate"][hf],
                           preferred_element_type=F32) for hf in range(2)]

    def rg_coeffs(self):
        half = C_DIM // 2
        decay = -RG_C * jax.nn.softplus(-self.p["rg_lambda"][...])
        for hf in range(2):
            cols = slice(hf * half, (hf + 1) * half)
            gz = self.gz[hf] + self.p["b_gate"][hf]
            r = _sigmoid(gz[:, :half])
            ig = _sigmoid(gz[:, half:])
            a = jnp.exp(r * decay[:, cols])
            t = 1.0 - a * a
            mult = jnp.where(t > 0.0, t * lax.rsqrt(t), 0.0)
            self.a[:, cols] = a
            self.bx[:, cols] = mult * (ig * self.xc[:, cols])

    def rg_scan(self):
        _segment_scan(self.a, self.bx, self.gate, self.yc, self.s["h_carry"])
        for q in range(C_DIM // LANES):
            for s in range(SUBLANES):
                c0 = A_DIM + B_DIM + q * LANES
                self.y[s * MIX_SEG:(s + 1) * MIX_SEG, c0:c0 + LANES] = (
                    self.yc[q, s * MIX_PITCH_OUT:s * MIX_PITCH_OUT + MIX_SEG, :].astype(BF16))

    def proj_out(self):
        self.out = jnp.dot(self.y[...], self.p["w_out"][...], preferred_element_type=F32)

    def finish(self):
        self.o_ref[self.rows, :] = _layer_norm(self.alpha * self.x + self.out, self.p["ln_g"][...],
                                               self.p["ln_b"][...])


_MIXER_PARAMS = ("w_in", "w_out", "ln_g", "ln_b", "sgu_g", "sgu_b", "sgu_w", "sgu_bias", "sconv_w",
                 "rg_conv_w", "rg_conv_b", "w_gate", "b_gate", "rg_lambda")
_MIXER_SCRATCH = ("wst", "cx_tail", "xr_hist", "h_carry", "cx", "gate", "xr", "xc", "a", "bx", "yc", "y")


def _mixer_kernel(alpha, steps_per_seq, x_ref, *refs):
    p = dict(zip(_MIXER_PARAMS, refs))
    o_ref = refs[len(_MIXER_PARAMS)]
    s = dict(zip(_MIXER_SCRATCH, refs[len(_MIXER_PARAMS) + 1:]))
    i = pl.program_id(0)

    @pl.when(i == 0)
    def _():
        t = lax.broadcasted_iota(jnp.int32, (A_HEADS * CHUNK, CHUNK), 0) % CHUNK
        src = lax.broadcasted_iota(jnp.int32, (A_HEADS * CHUNK, CHUNK), 1)
        s["wst"][...] = jnp.where(src <= t, p["sgu_w"][...], 0.0).astype(BF16)

    @pl.when(i % steps_per_seq == 0)
    def _():
        for name in ("cx_tail", "xr_hist", "h_carry"):
            s[name][...] = jnp.zeros_like(s[name])

    t0, t1 = (_SubTile(alpha, x_ref, o_ref, k, p, s) for k in range(MIX_SUB))
    for phase in (t0.proj_c, t0.proj_a, t0.proj_b, t0.rg_conv, t0.sgu_pre, t0.rg_gate_proj,
                  t0.sgu_mix, t0.short_conv,
                  t1.proj_c, t1.proj_a, t1.proj_b, t0.rg_coeffs, t0.rg_scan, t1.rg_conv, t1.sgu_pre,
                  t1.rg_gate_proj, t1.sgu_mix, t1.short_conv,
                  t0.proj_out, t1.rg_coeffs, t1.rg_scan, t1.proj_out, t0.finish, t1.finish):
        phase()


def _mixer(x, seq_len, layer, ln_idx, alpha, w_in, w_out, ln_g, ln_b, sgu_ln_g, sgu_ln_b, sgu_w,
           sgu_bias, sconv_w, rg_conv_w, rg_conv_b, w_gate, b_gate, rg_lambda):
    n = x.shape[0]
    tm = MIX_TILE
    step = tm * MIX_SUB
    assert seq_len % step == 0 and tm % CHUNK == 0
    row = lambda i: (i, 0)
    l2 = lambda i: (layer, 0, 0)
    l3 = lambda i: (layer, 0, 0, 0)
    lsel = lambda i: (ln_idx, 0, 0)
    half = C_DIM // 2
    strided_in = (MIX_SUB, C_DIM // LANES, SUBLANES * MIX_PITCH_IN, LANES)
    strided_out = (MIX_SUB, C_DIM // LANES, SUBLANES * MIX_PITCH_OUT, LANES)
    segment_order = (MIX_SUB, tm, C_DIM)
    return pl.pallas_call(
        functools.partial(_mixer_kernel, alpha, seq_len // step),
        grid=(n // step,),
        in_specs=[
            pl.BlockSpec((step, D_MODEL), row),
            _resident((None, D_MODEL, IN_COLS), l2),
            _resident((None, D_MODEL, D_MODEL), l2),
            _resident((None, 1, D_MODEL), lsel),
            _resident((None, 1, D_MODEL), lsel),
            _resident((None, 1, A_DIM), l2),
            _resident((None, 1, A_DIM), l2),
            _resident((None, A_HEADS * CHUNK, CHUNK), l2),
            _resident((None, CHUNK, A_DIM), l2),
            _resident((None, B_CONV, B_DIM), l2),
            _resident((None, C_CONV, C_DIM), l2),
            _resident((None, 1, C_DIM), l2),
            _resident((None, 2, half, 2 * half), l3),
            _resident((None, 2, 1, 2 * half), l3),
            _resident((None, 1, C_DIM), l2),
        ],
        out_specs=pl.BlockSpec((step, D_MODEL), row),
        out_shape=jax.ShapeDtypeStruct((n, D_MODEL), F32),
        scratch_shapes=[
            pltpu.VMEM((A_HEADS * CHUNK, CHUNK), BF16),
            pltpu.VMEM((SUBLANES, B_DIM), F32),
            pltpu.VMEM((C_CONV - 1, SUBLANES, C_DIM), F32),
            pltpu.VMEM((SUBLANES, C_DIM), F32),
            pltpu.VMEM((MIX_SUB, tm + SUBLANES, B_DIM), F32),
            pltpu.VMEM(strided_in, F32),
            pltpu.VMEM(strided_in, F32),
            pltpu.VMEM(segment_order, F32),
            pltpu.VMEM(segment_order, F32),
            pltpu.VMEM(segment_order, F32),
            pltpu.VMEM(strided_out, F32),
            pltpu.VMEM((MIX_SUB, tm, D_MODEL), BF16),
        ],
        compiler_params=pltpu.CompilerParams(
            dimension_semantics=("arbitrary",), vmem_limit_bytes=VMEM_LIMIT),
        name="mixer",
    )(x, w_in, w_out, ln_g, ln_b, sgu_ln_g, sgu_ln_b, sgu_w, sgu_bias, sconv_w, rg_conv_w,
      rg_conv_b, w_gate, b_gate, rg_lambda)


def _block_diag_gates(w_a, w_i, b_a, b_i):
    depth = w_a.shape[0]
    half_heads = C_HEADS // 2
    eye = jnp.eye(half_heads, dtype=w_a.dtype)

    def bd(w):
        w = w.reshape(depth, 2, half_heads, HEAD_DIM, HEAD_DIM)
        full = jnp.einsum("lghde,hk->lghdke", w, eye)
        return full.reshape(depth, 2, half_heads * HEAD_DIM, half_heads * HEAD_DIM)

    w = jnp.concatenate([bd(w_a), bd(w_i)], axis=-1)
    half = C_DIM // 2
    b = jnp.concatenate([b_a.reshape(depth, 2, 1, half), b_i.reshape(depth, 2, 1, half)], axis=-1)
    return w, b


def kernel(x, ln_g, ln_b, ffn_w_gate, ffn_w_up, ffn_w_down, w_in, sgu_ln_g, sgu_ln_b, sgu_w, sgu_b,
           sconv_w, rg_conv_w, rg_conv_b, rg_w_a, rg_b_a, rg_w_i, rg_b_i, rg_lambda, w_out):
    batch, seq_len, _ = x.shape
    depth = w_in.shape[0]
    alpha = (2.0 * depth) ** 0.25

    wg, wu, wd = (w.astype(BF16) for w in (ffn_w_gate, ffn_w_up, ffn_w_down))
    w_in_b, w_out_b = w_in.astype(BF16), w_out.astype(BF16)
    ln_g3 = ln_g.reshape(depth * 3, 1, D_MODEL)
    ln_b3 = ln_b.reshape(depth * 3, 1, D_MODEL)
    sgu_w2 = sgu_w.reshape(depth, A_HEADS * CHUNK, CHUNK)
    sgu_bias = jnp.repeat(jnp.swapaxes(sgu_b, 1, 2), HEAD_DIM, axis=-1)
    w_gate, b_gate = _block_diag_gates(rg_w_a, rg_w_i, rg_b_a, rg_b_i)
    w_gate = w_gate.astype(BF16)
    row3 = lambda p: p.reshape(depth, 1, p.shape[-1])

    h = x.reshape(batch * seq_len, D_MODEL)
    for l in range(depth):
        h = _ffn(h, wg, wu, wd, ln_g3, ln_b3, l, 0, 3 * l, alpha)
        h = _mixer(h, seq_len, l, 3 * l + 1, alpha, w_in_b, w_out_b, ln_g3, ln_b3,
                   row3(sgu_ln_g), row3(sgu_ln_b), sgu_w2, sgu_bias, sconv_w, rg_conv_w,
                   row3(rg_conv_b), w_gate, b_gate, row3(rg_lambda))
        h = _ffn(h, wg, wu, wd, ln_g3, ln_b3, l, 1, 3 * l + 2, alpha)
    return h.reshape(batch, seq_len, D_MODEL)
```

```python
import functools
import math

import jax
import jax.numpy as jnp
from jax import lax
from jax.experimental import pallas as pl
from jax.experimental.pallas import tpu as pltpu

D_MODEL = 1024
D_FF = 2816
HEAD_DIM = 64
A_HEADS = 4
A_DIM = A_HEADS * HEAD_DIM
CHUNK = 128
B_DIM = 256
B_CONV = 3
C_HEADS = 8
C_DIM = C_HEADS * HEAD_DIM
C_CONV = 4
RG_C = 8.0
IN_COLS = 2 * A_DIM + 3 * B_DIM + 2 * C_DIM
SPLIT_A = 2 * A_DIM
SPLIT_B = SPLIT_A + 3 * B_DIM
LN_EPS = 1e-5

SUBLANES = 8
LANES = 128
FFN_TILE = 256
FFN_SUB = 4
FFN_COLS = 256
MIX_TILE = 512
MIX_SUB = 2
MIX_SEG = MIX_TILE // SUBLANES
MIX_PITCH_IN = MIX_SEG + 4
MIX_PITCH_OUT = MIX_SEG + 8
LOG2E = math.log2(math.e)
VMEM_LIMIT = 60 * 1024 * 1024

F32 = jnp.float32
BF16 = jnp.bfloat16

_GELU_C0 = -2.0 * math.sqrt(2.0 / math.pi) * LOG2E
_GELU_C1 = _GELU_C0 * 0.044715


def _layer_norm(y, g, b):
    mu = jnp.mean(y, axis=-1, keepdims=True)
    yc = y - mu
    var = jnp.mean(yc * yc, axis=-1, keepdims=True)
    return yc * lax.rsqrt(var + LN_EPS) * g + b


def _gelu(x):
    return x / (1.0 + jnp.exp2(x * (_GELU_C0 + _GELU_C1 * (x * x))))


def _sigmoid(x):
    return 1.0 / (1.0 + jnp.exp2(x * (-LOG2E)))


def _resident(shape, index_map):
    return pl.BlockSpec(shape, index_map, pipeline_mode=pl.Buffered(1))


def _ffn_kernel(alpha, x_ref, wg_ref, wu_ref, wd_ref, g_ref, b_ref, o_ref, h_ref):
    for k in range(FFN_SUB):
        rows = slice(k * FFN_TILE, (k + 1) * FFN_TILE)
        x = x_ref[rows, :]
        xb = x.astype(BF16)
        for c in range(0, D_FF, FFN_COLS):
            hg = jnp.dot(xb, wg_ref[:, c:c + FFN_COLS], preferred_element_type=F32)
            hu = jnp.dot(xb, wu_ref[:, c:c + FFN_COLS], preferred_element_type=F32)
            h_ref[k, :, c:c + FFN_COLS] = (hg * _sigmoid(hg) * hu).astype(BF16)
        f = jnp.dot(h_ref[k], wd_ref[...], preferred_element_type=F32)
        o_ref[rows, :] = _layer_norm(alpha * x + 0.5 * f, g_ref[...], b_ref[...])


def _ffn(x, wg, wu, wd, ln_g, ln_b, layer, which, ln_idx, alpha):
    n = x.shape[0]
    step = FFN_TILE * FFN_SUB
    assert n % step == 0
    row = lambda i: (i, 0)
    wsel = lambda i: (layer, which, 0, 0)
    lsel = lambda i: (ln_idx, 0, 0)
    return pl.pallas_call(
        functools.partial(_ffn_kernel, alpha),
        grid=(n // step,),
        in_specs=[
            pl.BlockSpec((step, D_MODEL), row),
            _resident((None, None, D_MODEL, D_FF), wsel),
            _resident((None, None, D_MODEL, D_FF), wsel),
            _resident((None, None, D_FF, D_MODEL), wsel),
            _resident((None, 1, D_MODEL), lsel),
            _resident((None, 1, D_MODEL), lsel),
        ],
        out_specs=pl.BlockSpec((step, D_MODEL), row),
        out_shape=jax.ShapeDtypeStruct((n, D_MODEL), F32),
        scratch_shapes=[pltpu.VMEM((FFN_SUB, FFN_TILE, D_FF), BF16)],
        compiler_params=pltpu.CompilerParams(
            dimension_semantics=("arbitrary",), vmem_limit_bytes=VMEM_LIMIT),
        name="ffn",
    )(x, wg, wu, wd, ln_g, ln_b)


def _store_segments(ref, val):
    for q in range(val.shape[1] // LANES):
        for s in range(SUBLANES):
            ref[q, s * MIX_PITCH_IN:s * MIX_PITCH_IN + MIX_SEG, :] = (
                val[s * MIX_SEG:(s + 1) * MIX_SEG, q * LANES:(q + 1) * LANES])


def _segment_vector(ref, q, j):
    return ref.at[q][pl.ds(j, SUBLANES, stride=MIX_PITCH_IN), :]


def _segment_conv(xr_ref, hist_ref, w, bias, xc_ref):
    rowid = lax.broadcasted_iota(jnp.int32, (SUBLANES, LANES), 0)
    for q in range(C_DIM // LANES):
        lanes = slice(q * LANES, (q + 1) * LANES)
        wq = [w[k:k + 1, lanes] for k in range(C_CONV)]
        past = []
        for m in range(C_CONV - 1):
            cur = _segment_vector(xr_ref, q, MIX_SEG - (C_CONV - 1) + m)
            past.append(pltpu.roll(jnp.where(rowid == SUBLANES - 1, hist_ref[m, :, lanes], cur), 1, axis=0))
            hist_ref[m, :, lanes] = cur
        for j in range(MIX_SEG):
            x0 = _segment_vector(xr_ref, q, j)
            acc = wq[C_CONV - 1] * x0 + bias[:, lanes]
            for k in range(C_CONV - 1):
                acc = acc + wq[k] * past[k]
            xc_ref[j * SUBLANES:(j + 1) * SUBLANES, lanes] = acc
            past = past[1:] + [x0]


def _segment_scan(a_ref, bx_ref, gate_ref, yc_ref, carry_ref):
    rowid = lax.broadcasted_iota(jnp.int32, (SUBLANES, LANES), 0)

    for q in range(C_DIM // LANES):
        lanes = slice(q * LANES, (q + 1) * LANES)
        vec = lambda ref, j: ref[j * SUBLANES:(j + 1) * SUBLANES, lanes]
        p = vec(a_ref, 0)
        h = vec(bx_ref, 0)
        for j in range(1, MIX_SEG):
            a = vec(a_ref, j)
            h = a * h + vec(bx_ref, j)
            p = a * p
        for d in (1, 2, 4):
            keep = rowid >= d
            p_up = jnp.where(keep, pltpu.roll(p, d, axis=0), 1.0)
            h_up = jnp.where(keep, pltpu.roll(h, d, axis=0), 0.0)
            h = p * h_up + h
            p = p * p_up
        prev = carry_ref[:, lanes]
        end = p * prev + h
        carry_ref[:, lanes] = jnp.broadcast_to(end[SUBLANES - 1:SUBLANES, :], (SUBLANES, LANES))
        h = jnp.where(rowid >= 1, pltpu.roll(end, 1, axis=0), prev)
        for j in range(MIX_SEG):
            h = vec(a_ref, j) * h + vec(bx_ref, j)
            yc_ref.at[q][pl.ds(j, SUBLANES, stride=MIX_PITCH_OUT), :] = (
                h * _gelu(_segment_vector(gate_ref, q, j)))


class _SubTile:
    def __init__(self, alpha, x_ref, o_ref, k, p, s):
        self.alpha, self.o_ref, self.p, self.s = alpha, o_ref, p, s
        self.rows = slice(k * MIX_TILE, (k + 1) * MIX_TILE)
        self.x = x_ref[self.rows, :]
        self.xb = self.x.astype(BF16)
        for name in ("cx", "gate", "xr", "xc", "a", "bx", "yc", "y"):
            setattr(self, name, s[name].at[k])

    def proj_a(self):
        self.za = jnp.dot(self.xb, self.p["w_in"][:, 0:SPLIT_A], preferred_element_type=F32)

    def proj_b(self):
        self.zb = jnp.dot(self.xb, self.p["w_in"][:, SPLIT_A:SPLIT_B], preferred_element_type=F32)

    def proj_c(self):
        self.zc = jnp.dot(self.xb, self.p["w_in"][:, SPLIT_B:], preferred_element_type=F32)

    def sgu_pre(self):
        za = _gelu(self.za)
        self.u = za[:, :A_DIM]
        self.vb = _layer_norm(za[:, A_DIM:], self.p["sgu_g"][...], self.p["sgu_b"][...]).astype(BF16)

    def sgu_mix(self):
        lane = lax.broadcasted_iota(jnp.int32, (CHUNK, A_DIM), 1)
        for c in range(0, MIX_TILE, CHUNK):
            m = jnp.dot(self.s["wst"][...], self.vb[c:c + CHUNK, :], preferred_element_type=F32)
            mixed = m[0:CHUNK]
            for h in range(1, A_HEADS):
                mixed = jnp.where(lane >= h * HEAD_DIM, m[h * CHUNK:(h + 1) * CHUNK], mixed)
            mixed = mixed + self.p["sgu_bias"][...]
            self.y[c:c + CHUNK, 0:A_DIM] = (self.u[c:c + CHUNK] * mixed).astype(BF16)

    def short_conv(self):
        tm, zb, cx, tail = MIX_TILE, self.zb, self.cx, self.s["cx_tail"]
        cx[0:SUBLANES, :] = tail[...]
        cx[SUBLANES:SUBLANES + tm, :] = zb[:, B_DIM:2 * B_DIM] * zb[:, 2 * B_DIM:]
        w = self.p["sconv_w"][...]
        conv = w[B_CONV - 1:B_CONV] * cx[SUBLANES:SUBLANES + tm, :]
        for k in range(B_CONV - 1):
            off = SUBLANES - (B_CONV - 1) + k
            conv = conv + w[k:k + 1] * cx[off:off + tm, :]
        self.y[:, A_DIM:A_DIM + B_DIM] = (zb[:, :B_DIM] * conv).astype(BF16)
        tail[...] = cx[tm:tm + SUBLANES, :]

    def rg_conv(self):
        _store_segments(self.gate, self.zc[:, :C_DIM])
        _store_segments(self.xr, self.zc[:, C_DIM:])
        _segment_conv(self.xr, self.s["xr_hist"], self.p["rg_conv_w"][...], self.p["rg_conv_b"][...],
                      self.xc)

    def rg_gate_proj(self):
        half = C_DIM // 2
        xcb = self.xc[...].astype(BF16)
        self.gz = [jnp.dot(xcb[:, hf * half:(hf + 1) * half], self.p["w_gate"][hf],
                           preferred_element_type=F32) for hf in range(2)]

    def rg_coeffs(self):
        half = C_DIM // 2
        decay = -RG_C * jax.nn.softplus(-self.p["rg_lambda"][...])
        for hf in range(2):
            cols = slice(hf * half, (hf + 1) * half)
            gz = self.gz[hf] + self.p["b_gate"][hf]
            r = _sigmoid(gz[:, :half])
            ig = _sigmoid(gz[:, half:])
            a = jnp.exp(r * decay[:, cols])
            t = 1.0 - a * a
            mult = jnp.where(t > 0.0, t * lax.rsqrt(t), 0.0)
            self.a[:, cols] = a
            self.bx[:, cols] = mult * (ig * self.xc[:, cols])

    def rg_scan(self):
        _segment_scan(self.a, self.bx, self.gate, self.yc, self.s["h_carry"])
        for q in range(C_DIM // LANES):
            for s in range(SUBLANES):
                c0 = A_DIM + B_DIM + q * LANES
                self.y[s * MIX_SEG:(s + 1) * MIX_SEG, c0:c0 + LANES] = (
                    self.yc[q, s * MIX_PITCH_OUT:s * MIX_PITCH_OUT + MIX_SEG, :].astype(BF16))

    def proj_out(self):
        self.out = jnp.dot(self.y[...], self.p["w_out"][...], preferred_element_type=F32)

    def finish(self):
        self.o_ref[self.rows, :] = _layer_norm(self.alpha * self.x + self.out, self.p["ln_g"][...],
                                               self.p["ln_b"][...])


_MIXER_PARAMS = ("w_in", "w_out", "ln_g", "ln_b", "sgu_g", "sgu_b", "sgu_w", "sgu_bias", "sconv_w",
                 "rg_conv_w", "rg_conv_b", "w_gate", "b_gate", "rg_lambda")
_MIXER_SCRATCH = ("wst", "cx_tail", "xr_hist", "h_carry", "cx", "gate", "xr", "xc", "a", "bx", "yc", "y")


def _mixer_kernel(alpha, steps_per_seq, x_ref, *refs):
    p = dict(zip(_MIXER_PARAMS, refs))
    o_ref = refs[len(_MIXER_PARAMS)]
    s = dict(zip(_MIXER_SCRATCH, refs[len(_MIXER_PARAMS) + 1:]))
    i = pl.program_id(0)

    @pl.when(i == 0)
    def _():
        t = lax.broadcasted_iota(jnp.int32, (A_HEADS * CHUNK, CHUNK), 0) % CHUNK
        src = lax.broadcasted_iota(jnp.int32, (A_HEADS * CHUNK, CHUNK), 1)
        s["wst"][...] = jnp.where(src <= t, p["sgu_w"][...], 0.0).astype(BF16)

    @pl.when(i % steps_per_seq == 0)
    def _():
        for name in ("cx_tail", "xr_hist", "h_carry"):
            s[name][...] = jnp.zeros_like(s[name])

    t0, t1 = (_SubTile(alpha, x_ref, o_ref, k, p, s) for k in range(MIX_SUB))
    for phase in (t0.proj_c, t0.proj_a, t0.proj_b, t0.rg_conv, t0.sgu_pre, t0.rg_gate_proj,
                  t0.sgu_mix, t0.short_conv,
                  t1.proj_c, t1.proj_a, t1.proj_b, t0.rg_coeffs, t0.rg_scan, t1.rg_conv, t1.sgu_pre,
                  t1.rg_gate_proj, t1.sgu_mix, t1.short_conv,
                  t0.proj_out, t1.rg_coeffs, t1.rg_scan, t1.proj_out, t0.finish, t1.finish):
        phase()


def _mixer(x, seq_len, layer, ln_idx, alpha, w_in, w_out, ln_g, ln_b, sgu_ln_g, sgu_ln_b, sgu_w,
           sgu_bias, sconv_w, rg_conv_w, rg_conv_b, w_gate, b_gate, rg_lambda):
    n = x.shape[0]
    tm = MIX_TILE
    step = tm * MIX_SUB
    assert seq_len % step == 0 and tm % CHUNK == 0
    row = lambda i: (i, 0)
    l2 = lambda i: (layer, 0, 0)
    l3 = lambda i: (layer, 0, 0, 0)
    lsel = lambda i: (ln_idx, 0, 0)
    half = C_DIM // 2
    strided_in = (MIX_SUB, C_DIM // LANES, SUBLANES * MIX_PITCH_IN, LANES)
    strided_out = (MIX_SUB, C_DIM // LANES, SUBLANES * MIX_PITCH_OUT, LANES)
    segment_order = (MIX_SUB, tm, C_DIM)
    return pl.pallas_call(
        functools.partial(_mixer_kernel, alpha, seq_len // step),
        grid=(n // step,),
        in_specs=[
            pl.BlockSpec((step, D_MODEL), row),
            _resident((None, D_MODEL, IN_COLS), l2),
            _resident((None, D_MODEL, D_MODEL), l2),
            _resident((None, 1, D_MODEL), lsel),
            _resident((None, 1, D_MODEL), lsel),
            _resident((None, 1, A_DIM), l2),
            _resident((None, 1, A_DIM), l2),
            _resident((None, A_HEADS * CHUNK, CHUNK), l2),
            _resident((None, CHUNK, A_DIM), l2),
            _resident((None, B_CONV, B_DIM), l2),
            _resident((None, C_CONV, C_DIM), l2),
            _resident((None, 1, C_DIM), l2),
            _resident((None, 2, half, 2 * half), l3),
            _resident((None, 2, 1, 2 * half), l3),
            _resident((None, 1, C_DIM), l2),
        ],
        out_specs=pl.BlockSpec((step, D_MODEL), row),
        out_shape=jax.ShapeDtypeStruct((n, D_MODEL), F32),
        scratch_shapes=[
            pltpu.VMEM((A_HEADS * CHUNK, CHUNK), BF16),
            pltpu.VMEM((SUBLANES, B_DIM), F32),
            pltpu.VMEM((C_CONV - 1, SUBLANES, C_DIM), F32),
            pltpu.VMEM((SUBLANES, C_DIM), F32),
            pltpu.VMEM((MIX_SUB, tm + SUBLANES, B_DIM), F32),
            pltpu.VMEM(strided_in, F32),
            pltpu.VMEM(strided_in, F32),
            pltpu.VMEM(segment_order, F32),
            pltpu.VMEM(segment_order, F32),
            pltpu.VMEM(segment_order, F32),
            pltpu.VMEM(strided_out, F32),
            pltpu.VMEM((MIX_SUB, tm, D_MODEL), BF16),
        ],
        compiler_params=pltpu.CompilerParams(
            dimension_semantics=("arbitrary",), vmem_limit_bytes=VMEM_LIMIT),
        name="mixer",
    )(x, w_in, w_out, ln_g, ln_b, sgu_ln_g, sgu_ln_b, sgu_w, sgu_bias, sconv_w, rg_conv_w,
      rg_conv_b, w_gate, b_gate, rg_lambda)


def _block_diag_gates(w_a, w_i, b_a, b_i):
    depth = w_a.shape[0]
    half_heads = C_HEADS // 2
    eye = jnp.eye(half_heads, dtype=w_a.dtype)

    def bd(w):
        w = w.reshape(depth, 2, half_heads, HEAD_DIM, HEAD_DIM)
        full = jnp.einsum("lghde,hk->lghdke", w, eye)
        return full.reshape(depth, 2, half_heads * HEAD_DIM, half_heads * HEAD_DIM)

    w = jnp.concatenate([bd(w_a), bd(w_i)], axis=-1)
    half = C_DIM // 2
    b = jnp.concatenate([b_a.reshape(depth, 2, 1, half), b_i.reshape(depth, 2, 1, half)], axis=-1)
    return w, b


def kernel(x, ln_g, ln_b, ffn_w_gate, ffn_w_up, ffn_w_down, w_in, sgu_ln_g, sgu_ln_b, sgu_w, sgu_b,
           sconv_w, rg_conv_w, rg_conv_b, rg_w_a, rg_b_a, rg_w_i, rg_b_i, rg_lambda, w_out):
    batch, seq_len, _ = x.shape
    depth = w_in.shape[0]
    alpha = (2.0 * depth) ** 0.25

    wg, wu, wd = (w.astype(BF16) for w in (ffn_w_gate, ffn_w_up, ffn_w_down))
    w_in_b, w_out_b = w_in.astype(BF16), w_out.astype(BF16)
    ln_g3 = ln_g.reshape(depth * 3, 1, D_MODEL)
    ln_b3 = ln_b.reshape(depth * 3, 1, D_MODEL)
    sgu_w2 = sgu_w.reshape(depth, A_HEADS * CHUNK, CHUNK)
    sgu_bias = jnp.repeat(jnp.swapaxes(sgu_b, 1, 2), HEAD_DIM, axis=-1)
    w_gate, b_gate = _block_diag_gates(rg_w_a, rg_w_i, rg_b_a, rg_b_i)
    w_gate = w_gate.astype(BF16)
    row3 = lambda p: p.reshape(depth, 1, p.shape[-1])

    h = x.reshape(batch * seq_len, D_MODEL)
    for l in range(depth):
        h = _ffn(h, wg, wu, wd, ln_g3, ln_b3, l, 0, 3 * l, alpha)
        h = _mixer(h, seq_len, l, 3 * l + 1, alpha, w_in_b, w_out_b, ln_g3, ln_b3,
                   row3(sgu_ln_g), row3(sgu_ln_b), sgu_w2, sgu_bias, sconv_w, rg_conv_w,
                   row3(rg_conv_b), w_gate, b_gate, row3(rg_lambda))
        h = _ffn(h, wg, wu, wd, ln_g3, ln_b3, l, 1, 3 * l + 2, alpha)
    return h.reshape(batch, seq_len, D_MODEL)
```

```python
import functools
import math

import jax
import jax.numpy as jnp
from jax import lax
from jax.experimental import pallas as pl
from jax.experimental.pallas import tpu as pltpu

D_MODEL = 1024
D_FF = 2816
HEAD_DIM = 64
A_HEADS = 4
A_DIM = A_HEADS * HEAD_DIM
CHUNK = 128
B_DIM = 256
B_CONV = 3
C_HEADS = 8
C_DIM = C_HEADS * HEAD_DIM
C_CONV = 4
RG_C = 8.0
IN_COLS = 2 * A_DIM + 3 * B_DIM + 2 * C_DIM
SPLIT_A = 2 * A_DIM
SPLIT_B = SPLIT_A + 3 * B_DIM
LN_EPS = 1e-5

SUBLANES = 8
LANES = 128
FFN_TILE = 256
FFN_SUB = 4
FFN_COLS = 256
MIX_TILE = 512
MIX_SUB = 2
MIX_SEG = MIX_TILE // SUBLANES
MIX_PITCH_IN = MIX_SEG + 4
MIX_PITCH_OUT = MIX_SEG + 8
LOG2E = math.log2(math.e)
VMEM_LIMIT = 60 * 1024 * 1024

F32 = jnp.float32
BF16 = jnp.bfloat16

_GELU_C0 = -2.0 * math.sqrt(2.0 / math.pi) * LOG2E
_GELU_C1 = _GELU_C0 * 0.044715


def _layer_norm(y, g, b):
    mu = jnp.mean(y, axis=-1, keepdims=True)
    yc = y - mu
    var = jnp.mean(yc * yc, axis=-1, keepdims=True)
    return yc * lax.rsqrt(var + LN_EPS) * g + b


def _gelu(x):
    return x / (1.0 + jnp.exp2(x * (_GELU_C0 + _GELU_C1 * (x * x))))


def _sigmoid(x):
    return 1.0 / (1.0 + jnp.exp2(x * (-LOG2E)))


def _resident(shape, index_map):
    return pl.BlockSpec(shape, index_map, pipeline_mode=pl.Buffered(1))


def _ffn_kernel(alpha, x_ref, wg_ref, wu_ref, wd_ref, g_ref, b_ref, o_ref, h_ref):
    for k in range(FFN_SUB):
        rows = slice(k * FFN_TILE, (k + 1) * FFN_TILE)
        x = x_ref[rows, :]
        xb = x.astype(BF16)
        for c in range(0, D_FF, FFN_COLS):
            hg = jnp.dot(xb, wg_ref[:, c:c + FFN_COLS], preferred_element_type=F32)
            hu = jnp.dot(xb, wu_ref[:, c:c + FFN_COLS], preferred_element_type=F32)
            h_ref[k, :, c:c + FFN_COLS] = (hg * _sigmoid(hg) * hu).astype(BF16)
        f = jnp.dot(h_ref[k], wd_ref[...], preferred_element_type=F32)
        o_ref[rows, :] = _layer_norm(alpha * x + 0.5 * f, g_ref[...], b_ref[...])


def _ffn(x, wg, wu, wd, ln_g, ln_b, layer, which, ln_idx, alpha):
    n = x.shape[0]
    step = FFN_TILE * FFN_SUB
    assert n % step == 0
    row = lambda i: (i, 0)
    wsel = lambda i: (layer, which, 0, 0)
    lsel = lambda i: (ln_idx, 0, 0)
    return pl.pallas_call(
        functools.partial(_ffn_kernel, alpha),
        grid=(n // step,),
        in_specs=[
            pl.BlockSpec((step, D_MODEL), row),
            _resident((None, None, D_MODEL, D_FF), wsel),
            _resident((None, None, D_MODEL, D_FF), wsel),
            _resident((None, None, D_FF, D_MODEL), wsel),
            _resident((None, 1, D_MODEL), lsel),
            _resident((None, 1, D_MODEL), lsel),
        ],
        out_specs=pl.BlockSpec((step, D_MODEL), row),
        out_shape=jax.ShapeDtypeStruct((n, D_MODEL), F32),
        scratch_shapes=[pltpu.VMEM((FFN_SUB, FFN_TILE, D_FF), BF16)],
        compiler_params=pltpu.CompilerParams(
            dimension_semantics=("arbitrary",), vmem_limit_bytes=VMEM_LIMIT),
        name="ffn",
    )(x, wg, wu, wd, ln_g, ln_b)


def _store_segments(ref, val):
    for q in range(val.shape[1] // LANES):
        for s in range(SUBLANES):
            ref[q, s * MIX_PITCH_IN:s * MIX_PITCH_IN + MIX_SEG, :] = (
                val[s * MIX_SEG:(s + 1) * MIX_SEG, q * LANES:(q + 1) * LANES])


def _segment_vector(ref, q, j):
    return ref.at[q][pl.ds(j, SUBLANES, stride=MIX_PITCH_IN), :]


def _segment_conv(xr_ref, hist_ref, w, bias, xc_ref):
    rowid = lax.broadcasted_iota(jnp.int32, (SUBLANES, LANES), 0)
    for q in range(C_DIM // LANES):
        lanes = slice(q * LANES, (q + 1) * LANES)
        wq = [w[k:k + 1, lanes] for k in range(C_CONV)]
        past = []
        for m in range(C_CONV - 1):
            cur = _segment_vector(xr_ref, q, MIX_SEG - (C_CONV - 1) + m)
            past.append(pltpu.roll(jnp.where(rowid == SUBLANES - 1, hist_ref[m, :, lanes], cur), 1, axis=0))
            hist_ref[m, :, lanes] = cur
        for j in range(MIX_SEG):
            x0 = _segment_vector(xr_ref, q, j)
            acc = wq[C_CONV - 1] * x0 + bias[:, lanes]
            for k in range(C_CONV - 1):
                acc = acc + wq[k] * past[k]
            xc_ref[j * SUBLANES:(j + 1) * SUBLANES, lanes] = acc
            past = past[1:] + [x0]


def _segment_scan(a_ref, bx_ref, h_ref, carry_ref):
    rowid = lax.broadcasted_iota(jnp.int32, (SUBLANES, LANES), 0)

    for q in range(C_DIM // LANES):
        lanes = slice(q * LANES, (q + 1) * LANES)
        vec = lambda ref, j: ref[j * SUBLANES:(j + 1) * SUBLANES, lanes]
        p = vec(a_ref, 0)
        h = vec(bx_ref, 0)
        for j in range(1, MIX_SEG):
            a = vec(a_ref, j)
            h = a * h + vec(bx_ref, j)
            p = a * p
        for d in (1, 2, 4):
            keep = rowid >= d
            p_up = jnp.where(keep, pltpu.roll(p, d, axis=0), 1.0)
            h_up = jnp.where(keep, pltpu.roll(h, d, axis=0), 0.0)
            h = p * h_up + h
            p = p * p_up
        prev = carry_ref[:, lanes]
        end = p * prev + h
        carry_ref[:, lanes] = jnp.broadcast_to(end[SUBLANES - 1:SUBLANES, :], (SUBLANES, LANES))
        h = jnp.where(rowid >= 1, pltpu.roll(end, 1, axis=0), prev)
        for j in range(MIX_SEG):
            h = vec(a_ref, j) * h + vec(bx_ref, j)
            h_ref.at[q][pl.ds(j, SUBLANES, stride=MIX_PITCH_OUT), :] = h


class _SubTile:
    def __init__(self, alpha, x_ref, o_ref, k, p, s):
        self.alpha, self.o_ref, self.p, self.s = alpha, o_ref, p, s
        self.rows = slice(k * MIX_TILE, (k + 1) * MIX_TILE)
        self.x = x_ref[self.rows, :]
        self.xb = self.x.astype(BF16)
        for name in ("cx", "gg", "xr", "xc", "a", "bx", "h", "y"):
            setattr(self, name, s[name].at[k])

    def proj_a(self):
        self.za = jnp.dot(self.xb, self.p["w_in"][:, 0:SPLIT_A], preferred_element_type=F32)

    def proj_b(self):
        self.zb = jnp.dot(self.xb, self.p["w_in"][:, SPLIT_A:SPLIT_B], preferred_element_type=F32)

    def proj_c(self):
        self.zc = jnp.dot(self.xb, self.p["w_in"][:, SPLIT_B:], preferred_element_type=F32)

    def sgu_pre(self):
        za = _gelu(self.za)
        self.u = za[:, :A_DIM]
        self.vb = _layer_norm(za[:, A_DIM:], self.p["sgu_g"][...], self.p["sgu_b"][...]).astype(BF16)

    def sgu_mix(self):
        lane = lax.broadcasted_iota(jnp.int32, (CHUNK, A_DIM), 1)
        for c in range(0, MIX_TILE, CHUNK):
            m = jnp.dot(self.s["wst"][...], self.vb[c:c + CHUNK, :], preferred_element_type=F32)
            mixed = m[0:CHUNK]
            for h in range(1, A_HEADS):
                mixed = jnp.where(lane >= h * HEAD_DIM, m[h * CHUNK:(h + 1) * CHUNK], mixed)
            mixed = mixed + self.p["sgu_bias"][...]
            self.y[c:c + CHUNK, 0:A_DIM] = (self.u[c:c + CHUNK] * mixed).astype(BF16)

    def short_conv(self):
        tm, zb, cx, tail = MIX_TILE, self.zb, self.cx, self.s["cx_tail"]
        cx[0:SUBLANES, :] = tail[...]
        cx[SUBLANES:SUBLANES + tm, :] = zb[:, B_DIM:2 * B_DIM] * zb[:, 2 * B_DIM:]
        w = self.p["sconv_w"][...]
        conv = w[B_CONV - 1:B_CONV] * cx[SUBLANES:SUBLANES + tm, :]
        for k in range(B_CONV - 1):
            off = SUBLANES - (B_CONV - 1) + k
            conv = conv + w[k:k + 1] * cx[off:off + tm, :]
        self.y[:, A_DIM:A_DIM + B_DIM] = (zb[:, :B_DIM] * conv).astype(BF16)
        tail[...] = cx[tm:tm + SUBLANES, :]

    def rg_conv(self):
        self.gg[...] = _gelu(self.zc[:, :C_DIM])
        _store_segments(self.xr, self.zc[:, C_DIM:])
        _segment_conv(self.xr, self.s["xr_hist"], self.p["rg_conv_w"][...], self.p["rg_conv_b"][...],
                      self.xc)

    def rg_gate_proj(self):
        half = C_DIM // 2
        xcb = self.xc[...].astype(BF16)
        self.gz = [jnp.dot(xcb[:, hf * half:(hf + 1) * half], self.p["w_gate"][hf],
                           preferred_element_type=F32) for hf in range(2)]

    def rg_coeffs(self):
        half = C_DIM // 2
        decay = -RG_C * jax.nn.softplus(-self.p["rg_lambda"][...])
        for hf in range(2):
            cols = slice(hf * half, (hf + 1) * half)
            gz = self.gz[hf] + self.p["b_gate"][hf]
            r = _sigmoid(gz[:, :half])
            ig = _sigmoid(gz[:, half:])
            a = jnp.exp(r * decay[:, cols])
            t = 1.0 - a * a
            mult = jnp.where(t > 0.0, t * lax.rsqrt(t), 0.0)
            self.a[:, cols] = a
            self.bx[:, cols] = mult * (ig * self.xc[:, cols])

    def rg_scan(self):
        _segment_scan(self.a, self.bx, self.h, self.s["h_carry"])
        for q in range(C_DIM // LANES):
            for s in range(SUBLANES):
                rows = slice(s * MIX_SEG, (s + 1) * MIX_SEG)
                h = self.h[q, s * MIX_PITCH_OUT:s * MIX_PITCH_OUT + MIX_SEG, :]
                self.y[rows, A_DIM + B_DIM + q * LANES:A_DIM + B_DIM + (q + 1) * LANES] = (
                    self.gg[rows, q * LANES:(q + 1) * LANES] * h).astype(BF16)

    def proj_out(self):
        self.out = jnp.dot(self.y[...], self.p["w_out"][...], preferred_element_type=F32)

    def finish(self):
        self.o_ref[self.rows, :] = _layer_norm(self.alpha * self.x + self.out, self.p["ln_g"][...],
                                               self.p["ln_b"][...])


_MIXER_PARAMS = ("w_in", "w_out", "ln_g", "ln_b", "sgu_g", "sgu_b", "sgu_w", "sgu_bias", "sconv_w",
                 "rg_conv_w", "rg_conv_b", "w_gate", "b_gate", "rg_lambda")
_MIXER_SCRATCH = ("wst", "cx_tail", "xr_hist", "h_carry", "cx", "gg", "xr", "xc", "a", "bx", "h", "y")


def _mixer_kernel(alpha, steps_per_seq, x_ref, *refs):
    p = dict(zip(_MIXER_PARAMS, refs))
    o_ref = refs[len(_MIXER_PARAMS)]
    s = dict(zip(_MIXER_SCRATCH, refs[len(_MIXER_PARAMS) + 1:]))
    i = pl.program_id(0)

    @pl.when(i == 0)
    def _():
        t = lax.broadcasted_iota(jnp.int32, (A_HEADS * CHUNK, CHUNK), 0) % CHUNK
        src = lax.broadcasted_iota(jnp.int32, (A_HEADS * CHUNK, CHUNK), 1)
        s["wst"][...] = jnp.where(src <= t, p["sgu_w"][...], 0.0).astype(BF16)

    @pl.when(i % steps_per_seq == 0)
    def _():
        for name in ("cx_tail", "xr_hist", "h_carry"):
            s[name][...] = jnp.zeros_like(s[name])

    t0, t1 = (_SubTile(alpha, x_ref, o_ref, k, p, s) for k in range(MIX_SUB))
    for phase in (t0.proj_c, t0.proj_a, t0.proj_b, t0.rg_conv, t0.sgu_pre, t0.rg_gate_proj,
                  t0.sgu_mix, t0.short_conv,
                  t1.proj_c, t1.proj_a, t1.proj_b, t0.rg_coeffs, t0.rg_scan, t1.rg_conv, t1.sgu_pre,
                  t1.rg_gate_proj, t1.sgu_mix, t1.short_conv,
                  t0.proj_out, t1.rg_coeffs, t1.rg_scan, t1.proj_out, t0.finish, t1.finish):
        phase()


def _mixer(x, seq_len, layer, ln_idx, alpha, w_in, w_out, ln_g, ln_b, sgu_ln_g, sgu_ln_b, sgu_w,
           sgu_bias, sconv_w, rg_conv_w, rg_conv_b, w_gate, b_gate, rg_lambda):
    n = x.shape[0]
    tm = MIX_TILE
    step = tm * MIX_SUB
    assert seq_len % step == 0 and tm % CHUNK == 0
    row = lambda i: (i, 0)
    l2 = lambda i: (layer, 0, 0)
    l3 = lambda i: (layer, 0, 0, 0)
    lsel = lambda i: (ln_idx, 0, 0)
    half = C_DIM // 2
    strided_in = (MIX_SUB, C_DIM // LANES, SUBLANES * MIX_PITCH_IN, LANES)
    strided_out = (MIX_SUB, C_DIM // LANES, SUBLANES * MIX_PITCH_OUT, LANES)
    segment_order = (MIX_SUB, tm, C_DIM)
    return pl.pallas_call(
        functools.partial(_mixer_kernel, alpha, seq_len // step),
        grid=(n // step,),
        in_specs=[
            pl.BlockSpec((step, D_MODEL), row),
            _resident((None, D_MODEL, IN_COLS), l2),
            _resident((None, D_MODEL, D_MODEL), l2),
            _resident((None, 1, D_MODEL), lsel),
            _resident((None, 1, D_MODEL), lsel),
            _resident((None, 1, A_DIM), l2),
            _resident((None, 1, A_DIM), l2),
            _resident((None, A_HEADS * CHUNK, CHUNK), l2),
            _resident((None, CHUNK, A_DIM), l2),
            _resident((None, B_CONV, B_DIM), l2),
            _resident((None, C_CONV, C_DIM), l2),
            _resident((None, 1, C_DIM), l2),
            _resident((None, 2, half, 2 * half), l3),
            _resident((None, 2, 1, 2 * half), l3),
            _resident((None, 1, C_DIM), l2),
        ],
        out_specs=pl.BlockSpec((step, D_MODEL), row),
        out_shape=jax.ShapeDtypeStruct((n, D_MODEL), F32),
        scratch_shapes=[
            pltpu.VMEM((A_HEADS * CHUNK, CHUNK), BF16),
            pltpu.VMEM((SUBLANES, B_DIM), F32),
            pltpu.VMEM((C_CONV - 1, SUBLANES, C_DIM), F32),
            pltpu.VMEM((SUBLANES, C_DIM), F32),
            pltpu.VMEM((MIX_SUB, tm + SUBLANES, B_DIM), F32),
            pltpu.VMEM((MIX_SUB, tm, C_DIM), F32),
            pltpu.VMEM(strided_in, F32),
            pltpu.VMEM(segment_order, F32),
            pltpu.VMEM(segment_order, F32),
            pltpu.VMEM(segment_order, F32),
            pltpu.VMEM(strided_out, F32),
            pltpu.VMEM((MIX_SUB, tm, D_MODEL), BF16),
        ],
        compiler_params=pltpu.CompilerParams(
            dimension_semantics=("arbitrary",), vmem_limit_bytes=VMEM_LIMIT),
        name="mixer",
    )(x, w_in, w_out, ln_g, ln_b, sgu_ln_g, sgu_ln_b, sgu_w, sgu_bias, sconv_w, rg_conv_w,
      rg_conv_b, w_gate, b_gate, rg_lambda)


def _block_diag_gates(w_a, w_i, b_a, b_i):
    depth = w_a.shape[0]
    half_heads = C_HEADS // 2
    eye = jnp.eye(half_heads, dtype=w_a.dtype)

    def bd(w):
        w = w.reshape(depth, 2, half_heads, HEAD_DIM, HEAD_DIM)
        full = jnp.einsum("lghde,hk->lghdke", w, eye)
        return full.reshape(depth, 2, half_heads * HEAD_DIM, half_heads * HEAD_DIM)

    w = jnp.concatenate([bd(w_a), bd(w_i)], axis=-1)
    half = C_DIM // 2
    b = jnp.concatenate([b_a.reshape(depth, 2, 1, half), b_i.reshape(depth, 2, 1, half)], axis=-1)
    return w, b


def kernel(x, ln_g, ln_b, ffn_w_gate, ffn_w_up, ffn_w_down, w_in, sgu_ln_g, sgu_ln_b, sgu_w, sgu_b,
           sconv_w, rg_conv_w, rg_conv_b, rg_w_a, rg_b_a, rg_w_i, rg_b_i, rg_lambda, w_out):
    batch, seq_len, _ = x.shape
    depth = w_in.shape[0]
    alpha = (2.0 * depth) ** 0.25

    wg, wu, wd = (w.astype(BF16) for w in (ffn_w_gate, ffn_w_up, ffn_w_down))
    w_in_b, w_out_b = w_in.astype(BF16), w_out.astype(BF16)
    ln_g3 = ln_g.reshape(depth * 3, 1, D_MODEL)
    ln_b3 = ln_b.reshape(depth * 3, 1, D_MODEL)
    sgu_w2 = sgu_w.reshape(depth, A_HEADS * CHUNK, CHUNK)
    sgu_bias = jnp.repeat(jnp.swapaxes(sgu_b, 1, 2), HEAD_DIM, axis=-1)
    w_gate, b_gate = _block_diag_gates(rg_w_a, rg_w_i, rg_b_a, rg_b_i)
    w_gate = w_gate.astype(BF16)
    row3 = lambda p: p.reshape(depth, 1, p.shape[-1])

    h = x.reshape(batch * seq_len, D_MODEL)
    for l in range(depth):
        h = _ffn(h, wg, wu, wd, ln_g3, ln_b3, l, 0, 3 * l, alpha)
        h = _mixer(h, seq_len, l, 3 * l + 1, alpha, w_in_b, w_out_b, ln_g3, ln_b3,
                   row3(sgu_ln_g), row3(sgu_ln_b), sgu_w2, sgu_bias, sconv_w, rg_conv_w,
                   row3(rg_conv_b), w_gate, b_gate, row3(rg_lambda))
        h = _ffn(h, wg, wu, wd, ln_g3, ln_b3, l, 1, 3 * l + 2, alpha)
    return h.reshape(batch, seq_len, D_MODEL)
```

```python
import functools
import math

import jax
import jax.numpy as jnp
from jax import lax
from jax.experimental import pallas as pl
from jax.experimental.pallas import tpu as pltpu

D_MODEL = 1024
D_FF = 2816
HEAD_DIM = 64
A_HEADS = 4
A_DIM = A_HEADS * HEAD_DIM
CHUNK = 128
B_DIM = 256
B_CONV = 3
C_HEADS = 8
C_DIM = C_HEADS * HEAD_DIM
C_CONV = 4
RG_C = 8.0
IN_COLS = 2 * A_DIM + 3 * B_DIM + 2 * C_DIM
SPLIT_A = 2 * A_DIM
SPLIT_B = SPLIT_A + 3 * B_DIM
LN_EPS = 1e-5

SUBLANES = 8
LANES = 128
FFN_TILE = 256
FFN_SUB = 4
FFN_COLS = 256
MIX_TILE = 512
MIX_SUB = 2
MIX_SEG = MIX_TILE // SUBLANES
MIX_PITCH_IN = MIX_SEG + 4
MIX_PITCH_OUT = MIX_SEG + 4
LOG2E = math.log2(math.e)
VMEM_LIMIT = 60 * 1024 * 1024

F32 = jnp.float32
BF16 = jnp.bfloat16

_GELU_C0 = -2.0 * math.sqrt(2.0 / math.pi) * LOG2E
_GELU_C1 = _GELU_C0 * 0.044715


def _layer_norm(y, g, b):
    mu = jnp.mean(y, axis=-1, keepdims=True)
    yc = y - mu
    var = jnp.mean(yc * yc, axis=-1, keepdims=True)
    return yc * lax.rsqrt(var + LN_EPS) * g + b


def _gelu(x):
    return x / (1.0 + jnp.exp2(x * (_GELU_C0 + _GELU_C1 * (x * x))))


def _sigmoid(x):
    return 1.0 / (1.0 + jnp.exp2(x * (-LOG2E)))


def _resident(shape, index_map):
    return pl.BlockSpec(shape, index_map, pipeline_mode=pl.Buffered(1))


def _ffn_kernel(alpha, x_ref, wg_ref, wu_ref, wd_ref, g_ref, b_ref, o_ref, h_ref):
    for k in range(FFN_SUB):
        rows = slice(k * FFN_TILE, (k + 1) * FFN_TILE)
        x = x_ref[rows, :]
        xb = x.astype(BF16)
        for c in range(0, D_FF, FFN_COLS):
            hg = jnp.dot(xb, wg_ref[:, c:c + FFN_COLS], preferred_element_type=F32)
            hu = jnp.dot(xb, wu_ref[:, c:c + FFN_COLS], preferred_element_type=F32)
            h_ref[k, :, c:c + FFN_COLS] = (hg * _sigmoid(hg) * hu).astype(BF16)
        f = jnp.dot(h_ref[k], wd_ref[...], preferred_element_type=F32)
        o_ref[rows, :] = _layer_norm(alpha * x + 0.5 * f, g_ref[...], b_ref[...])


def _ffn(x, wg, wu, wd, ln_g, ln_b, layer, which, ln_idx, alpha):
    n = x.shape[0]
    step = FFN_TILE * FFN_SUB
    assert n % step == 0
    row = lambda i: (i, 0)
    wsel = lambda i: (layer, which, 0, 0)
    lsel = lambda i: (ln_idx, 0, 0)
    return pl.pallas_call(
        functools.partial(_ffn_kernel, alpha),
        grid=(n // step,),
        in_specs=[
            pl.BlockSpec((step, D_MODEL), row),
            _resident((None, None, D_MODEL, D_FF), wsel),
            _resident((None, None, D_MODEL, D_FF), wsel),
            _resident((None, None, D_FF, D_MODEL), wsel),
            _resident((None, 1, D_MODEL), lsel),
            _resident((None, 1, D_MODEL), lsel),
        ],
        out_specs=pl.BlockSpec((step, D_MODEL), row),
        out_shape=jax.ShapeDtypeStruct((n, D_MODEL), F32),
        scratch_shapes=[pltpu.VMEM((FFN_SUB, FFN_TILE, D_FF), BF16)],
        compiler_params=pltpu.CompilerParams(
            dimension_semantics=("arbitrary",), vmem_limit_bytes=VMEM_LIMIT),
        name="ffn",
    )(x, wg, wu, wd, ln_g, ln_b)


def _store_segments(ref, val):
    for q in range(val.shape[1] // LANES):
        for s in range(SUBLANES):
            ref[q, s * MIX_PITCH_IN:s * MIX_PITCH_IN + MIX_SEG, :] = (
                val[s * MIX_SEG:(s + 1) * MIX_SEG, q * LANES:(q + 1) * LANES])


def _segment_vector(ref, q, j):
    return ref.at[q][pl.ds(j, SUBLANES, stride=MIX_PITCH_IN), :]


def _segment_conv(xr_ref, hist_ref, w, bias, xc_ref):
    rowid = lax.broadcasted_iota(jnp.int32, (SUBLANES, LANES), 0)
    for q in range(C_DIM // LANES):
        lanes = slice(q * LANES, (q + 1) * LANES)
        wq = [w[k:k + 1, lanes] for k in range(C_CONV)]
        past = []
        for m in range(C_CONV - 1):
            cur = _segment_vector(xr_ref, q, MIX_SEG - (C_CONV - 1) + m)
            past.append(pltpu.roll(jnp.where(rowid == SUBLANES - 1, hist_ref[m, :, lanes], cur), 1, axis=0))
            hist_ref[m, :, lanes] = cur
        for j in range(MIX_SEG):
            x0 = _segment_vector(xr_ref, q, j)
            acc = wq[C_CONV - 1] * x0 + bias[:, lanes]
            for k in range(C_CONV - 1):
                acc = acc + wq[k] * past[k]
            xc_ref[j * SUBLANES:(j + 1) * SUBLANES, lanes] = acc
            past = past[1:] + [x0]


def _segment_scan(a_ref, bx_ref, h_ref, carry_ref):
    rowid = lax.broadcasted_iota(jnp.int32, (SUBLANES, LANES), 0)

    for q in range(C_DIM // LANES):
        lanes = slice(q * LANES, (q + 1) * LANES)
        vec = lambda ref, j: ref[j * SUBLANES:(j + 1) * SUBLANES, lanes]
        p = vec(a_ref, 0)
        h = vec(bx_ref, 0)
        for j in range(1, MIX_SEG):
            a = vec(a_ref, j)
            h = a * h + vec(bx_ref, j)
            p = a * p
        for d in (1, 2, 4):
            keep = rowid >= d
            p_up = jnp.where(keep, pltpu.roll(p, d, axis=0), 1.0)
            h_up = jnp.where(keep, pltpu.roll(h, d, axis=0), 0.0)
            h = p * h_up + h
            p = p * p_up
        prev = carry_ref[:, lanes]
        end = p * prev + h
        carry_ref[:, lanes] = jnp.broadcast_to(end[SUBLANES - 1:SUBLANES, :], (SUBLANES, LANES))
        h = jnp.where(rowid >= 1, pltpu.roll(end, 1, axis=0), prev)
        for j in range(MIX_SEG):
            h = vec(a_ref, j) * h + vec(bx_ref, j)
            h_ref.at[q][pl.ds(j, SUBLANES, stride=MIX_PITCH_OUT), :] = h


class _SubTile:
    def __init__(self, alpha, x_ref, o_ref, k, p, s):
        self.alpha, self.o_ref, self.p, self.s = alpha, o_ref, p, s
        self.rows = slice(k * MIX_TILE, (k + 1) * MIX_TILE)
        self.x = x_ref[self.rows, :]
        self.xb = self.x.astype(BF16)
        for name in ("cx", "gg", "xr", "xc", "a", "bx", "h", "y"):
            setattr(self, name, s[name].at[k])

    def proj_a(self):
        self.za = jnp.dot(self.xb, self.p["w_in"][:, 0:SPLIT_A], preferred_element_type=F32)

    def proj_b(self):
        self.zb = jnp.dot(self.xb, self.p["w_in"][:, SPLIT_A:SPLIT_B], preferred_element_type=F32)

    def proj_c(self):
        self.zc = jnp.dot(self.xb, self.p["w_in"][:, SPLIT_B:], preferred_element_type=F32)

    def sgu_pre(self):
        za = _gelu(self.za)
        self.u = za[:, :A_DIM]
        self.vb = _layer_norm(za[:, A_DIM:], self.p["sgu_g"][...], self.p["sgu_b"][...]).astype(BF16)

    def sgu_mix(self):
        lane = lax.broadcasted_iota(jnp.int32, (CHUNK, A_DIM), 1)
        for c in range(0, MIX_TILE, CHUNK):
            m = jnp.dot(self.s["wst"][...], self.vb[c:c + CHUNK, :], preferred_element_type=F32)
            mixed = m[0:CHUNK]
            for h in range(1, A_HEADS):
                mixed = jnp.where(lane >= h * HEAD_DIM, m[h * CHUNK:(h + 1) * CHUNK], mixed)
            mixed = mixed + self.p["sgu_bias"][...]
            self.y[c:c + CHUNK, 0:A_DIM] = (self.u[c:c + CHUNK] * mixed).astype(BF16)

    def short_conv(self):
        tm, zb, cx, tail = MIX_TILE, self.zb, self.cx, self.s["cx_tail"]
        cx[0:SUBLANES, :] = tail[...]
        cx[SUBLANES:SUBLANES + tm, :] = zb[:, B_DIM:2 * B_DIM] * zb[:, 2 * B_DIM:]
        w = self.p["sconv_w"][...]
        conv = w[B_CONV - 1:B_CONV] * cx[SUBLANES:SUBLANES + tm, :]
        for k in range(B_CONV - 1):
            off = SUBLANES - (B_CONV - 1) + k
            conv = conv + w[k:k + 1] * cx[off:off + tm, :]
        self.y[:, A_DIM:A_DIM + B_DIM] = (zb[:, :B_DIM] * conv).astype(BF16)
        tail[...] = cx[tm:tm + SUBLANES, :]

    def rg_conv(self):
        self.gg[...] = _gelu(self.zc[:, :C_DIM])
        _store_segments(self.xr, self.zc[:, C_DIM:])
        _segment_conv(self.xr, self.s["xr_hist"], self.p["rg_conv_w"][...], self.p["rg_conv_b"][...],
                      self.xc)

    def rg_gate_proj(self):
        half = C_DIM // 2
        xcb = self.xc[...].astype(BF16)
        self.gz = [jnp.dot(xcb[:, hf * half:(hf + 1) * half], self.p["w_gate"][hf],
                           preferred_element_type=F32) for hf in range(2)]

    def rg_coeffs(self):
        half = C_DIM // 2
        decay = -RG_C * jax.nn.softplus(-self.p["rg_lambda"][...])
        for hf in range(2):
            cols = slice(hf * half, (hf + 1) * half)
            gz = self.gz[hf] + self.p["b_gate"][hf]
            r = _sigmoid(gz[:, :half])
            ig = _sigmoid(gz[:, half:])
            a = jnp.exp(r * decay[:, cols])
            t = 1.0 - a * a
            mult = jnp.where(t > 0.0, t * lax.rsqrt(t), 0.0)
            self.a[:, cols] = a
            self.bx[:, cols] = mult * (ig * self.xc[:, cols])

    def rg_scan(self):
        _segment_scan(self.a, self.bx, self.h, self.s["h_carry"])
        for q in range(C_DIM // LANES):
            for s in range(SUBLANES):
                rows = slice(s * MIX_SEG, (s + 1) * MIX_SEG)
                h = self.h[q, s * MIX_PITCH_OUT:s * MIX_PITCH_OUT + MIX_SEG, :]
                self.y[rows, A_DIM + B_DIM + q * LANES:A_DIM + B_DIM + (q + 1) * LANES] = (
                    self.gg[rows, q * LANES:(q + 1) * LANES] * h).astype(BF16)

    def proj_out(self):
        self.out = jnp.dot(self.y[...], self.p["w_out"][...], preferred_element_type=F32)

    def finish(self):
        self.o_ref[self.rows, :] = _layer_norm(self.alpha * self.x + self.out, self.p["ln_g"][...],
                                               self.p["ln_b"][...])


_MIXER_PARAMS = ("w_in", "w_out", "ln_g", "ln_b", "sgu_g", "sgu_b", "sgu_w", "sgu_bias", "sconv_w",
                 "rg_conv_w", "rg_conv_b", "w_gate", "b_gate", "rg_lambda")
_MIXER_SCRATCH = ("wst", "cx_tail", "xr_hist", "h_carry", "cx", "gg", "xr", "xc", "a", "bx", "h", "y")


def _mixer_kernel(alpha, steps_per_seq, x_ref, *refs):
    p = dict(zip(_MIXER_PARAMS, refs))
    o_ref = refs[len(_MIXER_PARAMS)]
    s = dict(zip(_MIXER_SCRATCH, refs[len(_MIXER_PARAMS) + 1:]))
    i = pl.program_id(0)

    @pl.when(i == 0)
    def _():
        t = lax.broadcasted_iota(jnp.int32, (A_HEADS * CHUNK, CHUNK), 0) % CHUNK
        src = lax.broadcasted_iota(jnp.int32, (A_HEADS * CHUNK, CHUNK), 1)
        s["wst"][...] = jnp.where(src <= t, p["sgu_w"][...], 0.0).astype(BF16)

    @pl.when(i % steps_per_seq == 0)
    def _():
        for name in ("cx_tail", "xr_hist", "h_carry"):
            s[name][...] = jnp.zeros_like(s[name])

    t0, t1 = (_SubTile(alpha, x_ref, o_ref, k, p, s) for k in range(MIX_SUB))
    for phase in (t0.proj_c, t0.proj_a, t0.proj_b, t0.rg_conv, t0.sgu_pre, t0.rg_gate_proj,
                  t0.sgu_mix, t0.short_conv,
                  t1.proj_c, t1.proj_a, t1.proj_b, t0.rg_coeffs, t0.rg_scan, t1.rg_conv, t1.sgu_pre,
                  t1.rg_gate_proj, t1.sgu_mix, t1.short_conv,
                  t0.proj_out, t1.rg_coeffs, t1.rg_scan, t1.proj_out, t0.finish, t1.finish):
        phase()


def _mixer(x, seq_len, layer, ln_idx, alpha, w_in, w_out, ln_g, ln_b, sgu_ln_g, sgu_ln_b, sgu_w,
           sgu_bias, sconv_w, rg_conv_w, rg_conv_b, w_gate, b_gate, rg_lambda):
    n = x.shape[0]
    tm = MIX_TILE
    step = tm * MIX_SUB
    assert seq_len % step == 0 and tm % CHUNK == 0
    row = lambda i: (i, 0)
    l2 = lambda i: (layer, 0, 0)
    l3 = lambda i: (layer, 0, 0, 0)
    lsel = lambda i: (ln_idx, 0, 0)
    half = C_DIM // 2
    strided_in = (MIX_SUB, C_DIM // LANES, SUBLANES * MIX_PITCH_IN, LANES)
    strided_out = (MIX_SUB, C_DIM // LANES, SUBLANES * MIX_PITCH_OUT, LANES)
    segment_order = (MIX_SUB, tm, C_DIM)
    return pl.pallas_call(
        functools.partial(_mixer_kernel, alpha, seq_len // step),
        grid=(n // step,),
        in_specs=[
            pl.BlockSpec((step, D_MODEL), row),
            _resident((None, D_MODEL, IN_COLS), l2),
            _resident((None, D_MODEL, D_MODEL), l2),
            _resident((None, 1, D_MODEL), lsel),
            _resident((None, 1, D_MODEL), lsel),
            _resident((None, 1, A_DIM), l2),
            _resident((None, 1, A_DIM), l2),
            _resident((None, A_HEADS * CHUNK, CHUNK), l2),
            _resident((None, CHUNK, A_DIM), l2),
            _resident((None, B_CONV, B_DIM), l2),
            _resident((None, C_CONV, C_DIM), l2),
            _resident((None, 1, C_DIM), l2),
            _resident((None, 2, half, 2 * half), l3),
            _resident((None, 2, 1, 2 * half), l3),
            _resident((None, 1, C_DIM), l2),
        ],
        out_specs=pl.BlockSpec((step, D_MODEL), row),
        out_shape=jax.ShapeDtypeStruct((n, D_MODEL), F32),
        scratch_shapes=[
            pltpu.VMEM((A_HEADS * CHUNK, CHUNK), BF16),
            pltpu.VMEM((SUBLANES, B_DIM), F32),
            pltpu.VMEM((C_CONV - 1, SUBLANES, C_DIM), F32),
            pltpu.VMEM((SUBLANES, C_DIM), F32),
            pltpu.VMEM((MIX_SUB, tm + SUBLANES, B_DIM), F32),
            pltpu.VMEM((MIX_SUB, tm, C_DIM), F32),
            pltpu.VMEM(strided_in, F32),
            pltpu.VMEM(segment_order, F32),
            pltpu.VMEM(segment_order, F32),
            pltpu.VMEM(segment_order, F32),
            pltpu.VMEM(strided_out, F32),
            pltpu.VMEM((MIX_SUB, tm, D_MODEL), BF16),
        ],
        compiler_params=pltpu.CompilerParams(
            dimension_semantics=("arbitrary",), vmem_limit_bytes=VMEM_LIMIT),
        name="mixer",
    )(x, w_in, w_out, ln_g, ln_b, sgu_ln_g, sgu_ln_b, sgu_w, sgu_bias, sconv_w, rg_conv_w,
      rg_conv_b, w_gate, b_gate, rg_lambda)


def _block_diag_gates(w_a, w_i, b_a, b_i):
    depth = w_a.shape[0]
    half_heads = C_HEADS // 2
    eye = jnp.eye(half_heads, dtype=w_a.dtype)

    def bd(w):
        w = w.reshape(depth, 2, half_heads, HEAD_DIM, HEAD_DIM)
        full = jnp.einsum("lghde,hk->lghdke", w, eye)
        return full.reshape(depth, 2, half_heads * HEAD_DIM, half_heads * HEAD_DIM)

    w = jnp.concatenate([bd(w_a), bd(w_i)], axis=-1)
    half = C_DIM // 2
    b = jnp.concatenate([b_a.reshape(depth, 2, 1, half), b_i.reshape(depth, 2, 1, half)], axis=-1)
    return w, b


def kernel(x, ln_g, ln_b, ffn_w_gate, ffn_w_up, ffn_w_down, w_in, sgu_ln_g, sgu_ln_b, sgu_w, sgu_b,
           sconv_w, rg_conv_w, rg_conv_b, rg_w_a, rg_b_a, rg_w_i, rg_b_i, rg_lambda, w_out):
    batch, seq_len, _ = x.shape
    depth = w_in.shape[0]
    alpha = (2.0 * depth) ** 0.25

    wg, wu, wd = (w.astype(BF16) for w in (ffn_w_gate, ffn_w_up, ffn_w_down))
    w_in_b, w_out_b = w_in.astype(BF16), w_out.astype(BF16)
    ln_g3 = ln_g.reshape(depth * 3, 1, D_MODEL)
    ln_b3 = ln_b.reshape(depth * 3, 1, D_MODEL)
    sgu_w2 = sgu_w.reshape(depth, A_HEADS * CHUNK, CHUNK)
    sgu_bias = jnp.repeat(jnp.swapaxes(sgu_b, 1, 2), HEAD_DIM, axis=-1)
    w_gate, b_gate = _block_diag_gates(rg_w_a, rg_w_i, rg_b_a, rg_b_i)
    w_gate = w_gate.astype(BF16)
    row3 = lambda p: p.reshape(depth, 1, p.shape[-1])

    h = x.reshape(batch * seq_len, D_MODEL)
    for l in range(depth):
        h = _ffn(h, wg, wu, wd, ln_g3, ln_b3, l, 0, 3 * l, alpha)
        h = _mixer(h, seq_len, l, 3 * l + 1, alpha, w_in_b, w_out_b, ln_g3, ln_b3,
                   row3(sgu_ln_g), row3(sgu_ln_b), sgu_w2, sgu_bias, sconv_w, rg_conv_w,
                   row3(rg_conv_b), w_gate, b_gate, row3(rg_lambda))
        h = _ffn(h, wg, wu, wd, ln_g3, ln_b3, l, 1, 3 * l + 2, alpha)
    return h.reshape(batch, seq_len, D_MODEL)
```

```python
import functools
import math

import jax
import jax.numpy as jnp
from jax import lax
from jax.experimental import pallas as pl
from jax.experimental.pallas import tpu as pltpu

D_MODEL = 1024
D_FF = 2816
HEAD_DIM = 64
A_HEADS = 4
A_DIM = A_HEADS * HEAD_DIM
CHUNK = 128
B_DIM = 256
B_CONV = 3
C_HEADS = 8
C_DIM = C_HEADS * HEAD_DIM
C_CONV = 4
RG_C = 8.0
IN_COLS = 2 * A_DIM + 3 * B_DIM + 2 * C_DIM
SPLIT_A = 2 * A_DIM
SPLIT_B = SPLIT_A + 3 * B_DIM
LN_EPS = 1e-5

SUBLANES = 8
LANES = 128
FFN_TILE = 256
FFN_SUB = 8
FFN_HBUF = 2
FFN_COLS = 256
MIX_TILE = 512
MIX_SUB = 2
MIX_SEG = MIX_TILE // SUBLANES
MIX_PITCH_IN = MIX_SEG + 4
MIX_PITCH_OUT = MIX_SEG + 8
LOG2E = math.log2(math.e)
VMEM_LIMIT = 60 * 1024 * 1024

F32 = jnp.float32
BF16 = jnp.bfloat16

_GELU_C0 = -2.0 * math.sqrt(2.0 / math.pi) * LOG2E
_GELU_C1 = _GELU_C0 * 0.044715


def _layer_norm(y, g, b):
    mu = jnp.mean(y, axis=-1, keepdims=True)
    yc = y - mu
    var = jnp.mean(yc * yc, axis=-1, keepdims=True)
    return yc * lax.rsqrt(var + LN_EPS) * g + b


def _gelu(x):
    return x / (1.0 + jnp.exp2(x * (_GELU_C0 + _GELU_C1 * (x * x))))


def _sigmoid(x):
    return 1.0 / (1.0 + jnp.exp2(x * (-LOG2E)))


def _resident(shape, index_map):
    return pl.BlockSpec(shape, index_map, pipeline_mode=pl.Buffered(1))


def _ffn_kernel(alpha, x_ref, wg_ref, wu_ref, wd_ref, g_ref, b_ref, o_ref, h_ref):
    for k in range(FFN_SUB):
        rows = slice(k * FFN_TILE, (k + 1) * FFN_TILE)
        x = x_ref[rows, :]
        xb = x.astype(BF16)
        for c in range(0, D_FF, FFN_COLS):
            hg = jnp.dot(xb, wg_ref[:, c:c + FFN_COLS], preferred_element_type=F32)
            hu = jnp.dot(xb, wu_ref[:, c:c + FFN_COLS], preferred_element_type=F32)
            h_ref[k % FFN_HBUF, :, c:c + FFN_COLS] = (hg * _sigmoid(hg) * hu).astype(BF16)
        f = jnp.dot(h_ref[k % FFN_HBUF], wd_ref[...], preferred_element_type=F32)
        o_ref[rows, :] = _layer_norm(alpha * x + 0.5 * f, g_ref[...], b_ref[...])


def _ffn(x, wg, wu, wd, ln_g, ln_b, layer, which, ln_idx, alpha):
    n = x.shape[0]
    step = FFN_TILE * FFN_SUB
    assert n % step == 0
    row = lambda i: (i, 0)
    wsel = lambda i: (layer, which, 0, 0)
    lsel = lambda i: (ln_idx, 0, 0)
    return pl.pallas_call(
        functools.partial(_ffn_kernel, alpha),
        grid=(n // step,),
        in_specs=[
            pl.BlockSpec((step, D_MODEL), row),
            _resident((None, None, D_MODEL, D_FF), wsel),
            _resident((None, None, D_MODEL, D_FF), wsel),
            _resident((None, None, D_FF, D_MODEL), wsel),
            _resident((None, 1, D_MODEL), lsel),
            _resident((None, 1, D_MODEL), lsel),
        ],
        out_specs=pl.BlockSpec((step, D_MODEL), row),
        out_shape=jax.ShapeDtypeStruct((n, D_MODEL), F32),
        scratch_shapes=[pltpu.VMEM((FFN_HBUF, FFN_TILE, D_FF), BF16)],
        compiler_params=pltpu.CompilerParams(
            dimension_semantics=("arbitrary",), vmem_limit_bytes=VMEM_LIMIT),
        name="ffn",
    )(x, wg, wu, wd, ln_g, ln_b)


def _store_segments(ref, val):
    for q in range(val.shape[1] // LANES):
        for s in range(SUBLANES):
            ref[q, s * MIX_PITCH_IN:s * MIX_PITCH_IN + MIX_SEG, :] = (
                val[s * MIX_SEG:(s + 1) * MIX_SEG, q * LANES:(q + 1) * LANES])


def _segment_vector(ref, q, j):
    return ref.at[q][pl.ds(j, SUBLANES, stride=MIX_PITCH_IN), :]


def _segment_conv(xr_ref, hist_ref, w, bias, xc_ref):
    rowid = lax.broadcasted_iota(jnp.int32, (SUBLANES, LANES), 0)
    for q in range(C_DIM // LANES):
        lanes = slice(q * LANES, (q + 1) * LANES)
        wq = [w[k:k + 1, lanes] for k in range(C_CONV)]
        past = []
        for m in range(C_CONV - 1):
            cur = _segment_vector(xr_ref, q, MIX_SEG - (C_CONV - 1) + m)
            past.append(pltpu.roll(jnp.where(rowid == SUBLANES - 1, hist_ref[m, :, lanes], cur), 1, axis=0))
            hist_ref[m, :, lanes] = cur
        for j in range(MIX_SEG):
            x0 = _segment_vector(xr_ref, q, j)
            acc = wq[C_CONV - 1] * x0 + bias[:, lanes]
            for k in range(C_CONV - 1):
                acc = acc + wq[k] * past[k]
            xc_ref[j * SUBLANES:(j + 1) * SUBLANES, lanes] = acc
            past = past[1:] + [x0]


def _segment_scan(a_ref, bx_ref, h_ref, carry_ref):
    rowid = lax.broadcasted_iota(jnp.int32, (SUBLANES, LANES), 0)

    for q in range(C_DIM // LANES):
        lanes = slice(q * LANES, (q + 1) * LANES)
        vec = lambda ref, j: ref[j * SUBLANES:(j + 1) * SUBLANES, lanes]
        p = vec(a_ref, 0)
        h = vec(bx_ref, 0)
        for j in range(1, MIX_SEG):
            a = vec(a_ref, j)
            h = a * h + vec(bx_ref, j)
            p = a * p
        for d in (1, 2, 4):
            keep = rowid >= d
            p_up = jnp.where(keep, pltpu.roll(p, d, axis=0), 1.0)
            h_up = jnp.where(keep, pltpu.roll(h, d, axis=0), 0.0)
            h = p * h_up + h
            p = p * p_up
        prev = carry_ref[:, lanes]
        end = p * prev + h
        carry_ref[:, lanes] = jnp.broadcast_to(end[SUBLANES - 1:SUBLANES, :], (SUBLANES, LANES))
        h = jnp.where(rowid >= 1, pltpu.roll(end, 1, axis=0), prev)
        for j in range(MIX_SEG):
            h = vec(a_ref, j) * h + vec(bx_ref, j)
            h_ref.at[q][pl.ds(j, SUBLANES, stride=MIX_PITCH_OUT), :] = h


class _SubTile:
    def __init__(self, alpha, x_ref, o_ref, k, p, s):
        self.alpha, self.o_ref, self.p, self.s = alpha, o_ref, p, s
        self.rows = slice(k * MIX_TILE, (k + 1) * MIX_TILE)
        self.x = x_ref[self.rows, :]
        self.xb = self.x.astype(BF16)
        for name in ("cx", "gg", "xr", "xc", "a", "bx", "h", "y"):
            setattr(self, name, s[name].at[k])

    def proj_a(self):
        self.za = jnp.dot(self.xb, self.p["w_in"][:, 0:SPLIT_A], preferred_element_type=F32)

    def proj_b(self):
        self.zb = jnp.dot(self.xb, self.p["w_in"][:, SPLIT_A:SPLIT_B], preferred_element_type=F32)

    def proj_c(self):
        self.zc = jnp.dot(self.xb, self.p["w_in"][:, SPLIT_B:], preferred_element_type=F32)

    def sgu_pre(self):
        za = _gelu(self.za)
        self.u = za[:, :A_DIM]
        self.vb = _layer_norm(za[:, A_DIM:], self.p["sgu_g"][...], self.p["sgu_b"][...]).astype(BF16)

    def sgu_mix(self):
        lane = lax.broadcasted_iota(jnp.int32, (CHUNK, A_DIM), 1)
        for c in range(0, MIX_TILE, CHUNK):
            m = jnp.dot(self.s["wst"][...], self.vb[c:c + CHUNK, :], preferred_element_type=F32)
            mixed = m[0:CHUNK]
            for h in range(1, A_HEADS):
                mixed = jnp.where(lane >= h * HEAD_DIM, m[h * CHUNK:(h + 1) * CHUNK], mixed)
            mixed = mixed + self.p["sgu_bias"][...]
            self.y[c:c + CHUNK, 0:A_DIM] = (self.u[c:c + CHUNK] * mixed).astype(BF16)

    def short_conv(self):
        tm, zb, cx, tail = MIX_TILE, self.zb, self.cx, self.s["cx_tail"]
        cx[0:SUBLANES, :] = tail[...]
        cx[SUBLANES:SUBLANES + tm, :] = zb[:, B_DIM:2 * B_DIM] * zb[:, 2 * B_DIM:]
        w = self.p["sconv_w"][...]
        conv = w[B_CONV - 1:B_CONV] * cx[SUBLANES:SUBLANES + tm, :]
        for k in range(B_CONV - 1):
            off = SUBLANES - (B_CONV - 1) + k
            conv = conv + w[k:k + 1] * cx[off:off + tm, :]
        self.y[:, A_DIM:A_DIM + B_DIM] = (zb[:, :B_DIM] * conv).astype(BF16)
        tail[...] = cx[tm:tm + SUBLANES, :]

    def rg_conv(self):
        self.gg[...] = _gelu(self.zc[:, :C_DIM])
        _store_segments(self.xr, self.zc[:, C_DIM:])
        _segment_conv(self.xr, self.s["xr_hist"], self.p["rg_conv_w"][...], self.p["rg_conv_b"][...],
                      self.xc)

    def rg_gate_proj(self):
        half = C_DIM // 2
        xcb = self.xc[...].astype(BF16)
        self.gz = [jnp.dot(xcb[:, hf * half:(hf + 1) * half], self.p["w_gate"][hf],
                           preferred_element_type=F32) for hf in range(2)]

    def rg_coeffs(self):
        half = C_DIM // 2
        decay = -RG_C * jax.nn.softplus(-self.p["rg_lambda"][...])
        for hf in range(2):
            cols = slice(hf * half, (hf + 1) * half)
            gz = self.gz[hf] + self.p["b_gate"][hf]
            r = _sigmoid(gz[:, :half])
            ig = _sigmoid(gz[:, half:])
            a = jnp.exp(r * decay[:, cols])
            t = 1.0 - a * a
            mult = jnp.where(t > 0.0, t * lax.rsqrt(t), 0.0)
            self.a[:, cols] = a
            self.bx[:, cols] = mult * (ig * self.xc[:, cols])

    def rg_scan(self):
        _segment_scan(self.a, self.bx, self.h, self.s["h_carry"])
        for q in range(C_DIM // LANES):
            for s in range(SUBLANES):
                rows = slice(s * MIX_SEG, (s + 1) * MIX_SEG)
                h = self.h[q, s * MIX_PITCH_OUT:s * MIX_PITCH_OUT + MIX_SEG, :]
                self.y[rows, A_DIM + B_DIM + q * LANES:A_DIM + B_DIM + (q + 1) * LANES] = (
                    self.gg[rows, q * LANES:(q + 1) * LANES] * h).astype(BF16)

    def proj_out(self):
        self.out = jnp.dot(self.y[...], self.p["w_out"][...], preferred_element_type=F32)

    def finish(self):
        self.o_ref[self.rows, :] = _layer_norm(self.alpha * self.x + self.out, self.p["ln_g"][...],
                                               self.p["ln_b"][...])


_MIXER_PARAMS = ("w_in", "w_out", "ln_g", "ln_b", "sgu_g", "sgu_b", "sgu_w", "sgu_bias", "sconv_w",
                 "rg_conv_w", "rg_conv_b", "w_gate", "b_gate", "rg_lambda")
_MIXER_SCRATCH = ("wst", "cx_tail", "xr_hist", "h_carry", "cx", "gg", "xr", "xc", "a", "bx", "h", "y")


def _mixer_kernel(alpha, steps_per_seq, x_ref, *refs):
    p = dict(zip(_MIXER_PARAMS, refs))
    o_ref = refs[len(_MIXER_PARAMS)]
    s = dict(zip(_MIXER_SCRATCH, refs[len(_MIXER_PARAMS) + 1:]))
    i = pl.program_id(0)

    @pl.when(i == 0)
    def _():
        t = lax.broadcasted_iota(jnp.int32, (A_HEADS * CHUNK, CHUNK), 0) % CHUNK
        src = lax.broadcasted_iota(jnp.int32, (A_HEADS * CHUNK, CHUNK), 1)
        s["wst"][...] = jnp.where(src <= t, p["sgu_w"][...], 0.0).astype(BF16)

    @pl.when(i % steps_per_seq == 0)
    def _():
        for name in ("cx_tail", "xr_hist", "h_carry"):
            s[name][...] = jnp.zeros_like(s[name])

    t0, t1 = (_SubTile(alpha, x_ref, o_ref, k, p, s) for k in range(MIX_SUB))
    for phase in (t0.proj_c, t0.proj_a, t0.proj_b, t0.rg_conv, t0.sgu_pre, t0.rg_gate_proj,
                  t0.sgu_mix, t0.short_conv,
                  t1.proj_c, t1.proj_a, t1.proj_b, t0.rg_coeffs, t0.rg_scan, t1.rg_conv, t1.sgu_pre,
                  t1.rg_gate_proj, t1.sgu_mix, t1.short_conv,
                  t0.proj_out, t1.rg_coeffs, t1.rg_scan, t1.proj_out, t0.finish, t1.finish):
        phase()


def _mixer(x, seq_len, layer, ln_idx, alpha, w_in, w_out, ln_g, ln_b, sgu_ln_g, sgu_ln_b, sgu_w,
           sgu_bias, sconv_w, rg_conv_w, rg_conv_b, w_gate, b_gate, rg_lambda):
    n = x.shape[0]
    tm = MIX_TILE
    step = tm * MIX_SUB
    assert seq_len % step == 0 and tm % CHUNK == 0
    row = lambda i: (i, 0)
    l2 = lambda i: (layer, 0, 0)
    l3 = lambda i: (layer, 0, 0, 0)
    lsel = lambda i: (ln_idx, 0, 0)
    half = C_DIM // 2
    strided_in = (MIX_SUB, C_DIM // LANES, SUBLANES * MIX_PITCH_IN, LANES)
    strided_out = (MIX_SUB, C_DIM // LANES, SUBLANES * MIX_PITCH_OUT, LANES)
    segment_order = (MIX_SUB, tm, C_DIM)
    return pl.pallas_call(
        functools.partial(_mixer_kernel, alpha, seq_len // step),
        grid=(n // step,),
        in_specs=[
            pl.BlockSpec((step, D_MODEL), row),
            _resident((None, D_MODEL, IN_COLS), l2),
            _resident((None, D_MODEL, D_MODEL), l2),
            _resident((None, 1, D_MODEL), lsel),
            _resident((None, 1, D_MODEL), lsel),
            _resident((None, 1, A_DIM), l2),
            _resident((None, 1, A_DIM), l2),
            _resident((None, A_HEADS * CHUNK, CHUNK), l2),
            _resident((None, CHUNK, A_DIM), l2),
            _resident((None, B_CONV, B_DIM), l2),
            _resident((None, C_CONV, C_DIM), l2),
            _resident((None, 1, C_DIM), l2),
            _resident((None, 2, half, 2 * half), l3),
            _resident((None, 2, 1, 2 * half), l3),
            _resident((None, 1, C_DIM), l2),
        ],
        out_specs=pl.BlockSpec((step, D_MODEL), row),
        out_shape=jax.ShapeDtypeStruct((n, D_MODEL), F32),
        scratch_shapes=[
            pltpu.VMEM((A_HEADS * CHUNK, CHUNK), BF16),
            pltpu.VMEM((SUBLANES, B_DIM), F32),
            pltpu.VMEM((C_CONV - 1, SUBLANES, C_DIM), F32),
            pltpu.VMEM((SUBLANES, C_DIM), F32),
            pltpu.VMEM((MIX_SUB, tm + SUBLANES, B_DIM), F32),
            pltpu.VMEM((MIX_SUB, tm, C_DIM), F32),
            pltpu.VMEM(strided_in, F32),
            pltpu.VMEM(segment_order, F32),
            pltpu.VMEM(segment_order, F32),
            pltpu.VMEM(segment_order, F32),
            pltpu.VMEM(strided_out, F32),
            pltpu.VMEM((MIX_SUB, tm, D_MODEL), BF16),
        ],
        compiler_params=pltpu.CompilerParams(
            dimension_semantics=("arbitrary",), vmem_limit_bytes=VMEM_LIMIT),
        name="mixer",
    )(x, w_in, w_out, ln_g, ln_b, sgu_ln_g, sgu_ln_b, sgu_w, sgu_bias, sconv_w, rg_conv_w,
      rg_conv_b, w_gate, b_gate, rg_lambda)


def _block_diag_gates(w_a, w_i, b_a, b_i):
    depth = w_a.shape[0]
    half_heads = C_HEADS // 2
    eye = jnp.eye(half_heads, dtype=w_a.dtype)

    def bd(w):
        w = w.reshape(depth, 2, half_heads, HEAD_DIM, HEAD_DIM)
        full = jnp.einsum("lghde,hk->lghdke", w, eye)
        return full.reshape(depth, 2, half_heads * HEAD_DIM, half_heads * HEAD_DIM)

    w = jnp.concatenate([bd(w_a), bd(w_i)], axis=-1)
    half = C_DIM // 2
    b = jnp.concatenate([b_a.reshape(depth, 2, 1, half), b_i.reshape(depth, 2, 1, half)], axis=-1)
    return w, b


def kernel(x, ln_g, ln_b, ffn_w_gate, ffn_w_up, ffn_w_down, w_in, sgu_ln_g, sgu_ln_b, sgu_w, sgu_b,
           sconv_w, rg_conv_w, rg_conv_b, rg_w_a, rg_b_a, rg_w_i, rg_b_i, rg_lambda, w_out):
    batch, seq_len, _ = x.shape
    depth = w_in.shape[0]
    alpha = (2.0 * depth) ** 0.25

    wg, wu, wd = (w.astype(BF16) for w in (ffn_w_gate, ffn_w_up, ffn_w_down))
    w_in_b, w_out_b = w_in.astype(BF16), w_out.astype(BF16)
    ln_g3 = ln_g.reshape(depth * 3, 1, D_MODEL)
    ln_b3 = ln_b.reshape(depth * 3, 1, D_MODEL)
    sgu_w2 = sgu_w.reshape(depth, A_HEADS * CHUNK, CHUNK)
    sgu_bias = jnp.repeat(jnp.swapaxes(sgu_b, 1, 2), HEAD_DIM, axis=-1)
    w_gate, b_gate = _block_diag_gates(rg_w_a, rg_w_i, rg_b_a, rg_b_i)
    w_gate = w_gate.astype(BF16)
    row3 = lambda p: p.reshape(depth, 1, p.shape[-1])

    h = x.reshape(batch * seq_len, D_MODEL)
    for l in range(depth):
        h = _ffn(h, wg, wu, wd, ln_g3, ln_b3, l, 0, 3 * l, alpha)
        h = _mixer(h, seq_len, l, 3 * l + 1, alpha, w_in_b, w_out_b, ln_g3, ln_b3,
                   row3(sgu_ln_g), row3(sgu_ln_b), sgu_w2, sgu_bias, sconv_w, rg_conv_w,
                   row3(rg_conv_b), w_gate, b_gate, row3(rg_lambda))
        h = _ffn(h, wg, wu, wd, ln_g3, ln_b3, l, 1, 3 * l + 2, alpha)
    return h.reshape(batch, seq_len, D_MODEL)
```
